```python
import jax, jax.numpy as jnp
from jax import lax
import numpy as np

D_MODEL = 1024
BATCH = 16
SEQ = 2048
DEPTH = 1

PLE_DIM = 256
CONV_WIDTH = 512
CONV_GROUPS = 8
CONV_K = 3
N_HEADS = 8
N_KV_HEADS = 2
HEAD_DIM = 64
GROUP_SIZE = N_HEADS // N_KV_HEADS
ATTN_WIDTH = N_HEADS * HEAD_DIM
KV_WIDTH = N_KV_HEADS * HEAD_DIM
MIX_WIDTH = CONV_WIDTH + ATTN_WIDTH
N_BRANCH = 3
IN_COLS = 3 * CONV_WIDTH + ATTN_WIDTH + 2 * N_BRANCH * KV_WIDTH + N_BRANCH * N_HEADS
CMP_LEN = 32
CMP_STRIDE = 16
SEL_BLOCK = 64
N_SEL = 8
WINDOW = 512
Q_BLOCK = 128
D_FF = 2816
FFN_K = 3
ROPE_THETA = 10000.0
EPS = 1e-6
NEG = -1e30
SEL_FORCE = 1e4

kernel_name = 'hybrid_shortconv_nsa_convffn_ple'


def rms_norm(x, g):
    xf = x.astype(jnp.float32)
    y = xf * lax.rsqrt(jnp.mean(xf * xf, axis=-1, keepdims=True) + EPS)
    return (y * g.astype(jnp.float32)).astype(x.dtype)


def causal_dwconv(x, w):
    k, c = w.shape
    return lax.conv_general_dilated(
        x, w[:, None, :].astype(x.dtype), window_strides=(1,), padding=[(k - 1, 0)],
        dimension_numbers=('NWC', 'WIO', 'NWC'), feature_group_count=c)


def rope_tables(pos):
    inv = ROPE_THETA ** (-jnp.arange(0, HEAD_DIM, 2, dtype=jnp.float32) / HEAD_DIM)
    ang = pos.astype(jnp.float32)[:, None] * inv[None, :]
    ang = jnp.concatenate([ang, ang], axis=-1)
    return jnp.cos(ang), jnp.sin(ang)


def apply_rope(x, cos, sin):
    x1, x2 = jnp.split(x, 2, axis=-1)
    rot = jnp.concatenate([-x2, x1], axis=-1)
    return (x * cos + rot * sin).astype(x.dtype)


def masked_softmax(s, mask):
    s = jnp.where(mask, s.astype(jnp.float32), NEG)
    m = jnp.max(s, axis=-1, keepdims=True)
    e = jnp.where(mask, jnp.exp(s - m), 0.0)
    return e / jnp.maximum(jnp.sum(e, axis=-1, keepdims=True), 1e-30)


def short_conv_mixer(x_in, b_gate, c_gate, w):
    return b_gate * causal_dwconv(c_gate * x_in, w)


def compress_kv(kv, cmp_idx, pe, w1, w2):
    blocks = kv[:, :, cmp_idx] + pe.astype(kv.dtype)
    flat = blocks.reshape(blocks.shape[:3] + (CMP_LEN * HEAD_DIM,))
    return jax.nn.silu(flat @ w1) @ w2


def native_sparse_attention(q, kc, vc, ks, vs, kw, vw, gates, cmp_end, overlap):
    b, s = q.shape[:2]
    n_blk = s // SEL_BLOCK
    n_sel = min(N_SEL, n_blk)
    n_qb = s // Q_BLOCK
    scale = HEAD_DIM ** -0.5
    qg = q.reshape(b, s, N_KV_HEADS, GROUP_SIZE, HEAD_DIM).transpose(0, 2, 3, 1, 4)
    gg = gates.reshape(b, s, N_KV_HEADS, GROUP_SIZE, N_BRANCH).transpose(0, 2, 3, 1, 4)
    ks_blk = ks.reshape(b, N_KV_HEADS, n_blk, SEL_BLOCK, HEAD_DIM)
    vs_blk = vs.reshape(b, N_KV_HEADS, n_blk, SEL_BLOCK, HEAD_DIM)
    kw_pad = jnp.pad(kw, ((0, 0), (0, 0), (WINDOW, 0), (0, 0)))
    vw_pad = jnp.pad(vw, ((0, 0), (0, 0), (WINDOW, 0), (0, 0)))
    blk_ids = jnp.arange(n_blk)
    gather = jax.vmap(jax.vmap(lambda blocks, ix: blocks[ix]))

    def query_block(i):
        q0 = i * Q_BLOCK
        t = q0 + jnp.arange(Q_BLOCK)
        qb = lax.dynamic_slice_in_dim(qg, q0, Q_BLOCK, axis=3)
        gb = lax.dynamic_slice_in_dim(gg, q0, Q_BLOCK, axis=3).astype(jnp.float32)
        s_c = jnp.einsum('bhgqd,bhcd->bhgqc', qb, kc) * scale
        p_c = masked_softmax(s_c, cmp_end[None, :] <= t[:, None])
        o_c = jnp.einsum('bhgqc,bhcd->bhgqd', p_c.astype(vc.dtype), vc)
        imp = jnp.einsum('bhqc,cn->bhqn', p_c.sum(axis=2), overlap)
        cur = t // SEL_BLOCK
        valid = blk_ids[None, :] * SEL_BLOCK <= t[:, None]
        forced = (blk_ids[None, :] == 0) | (blk_ids[None, :] == cur[:, None]) | (blk_ids[None, :] == cur[:, None] - 1)
        score = jnp.where(valid, imp + jnp.where(forced, SEL_FORCE, 0.0), NEG)
        _, idx = lax.top_k(score, n_sel)
        k_g = gather(ks_blk, idx).reshape(b, N_KV_HEADS, Q_BLOCK, n_sel * SEL_BLOCK, HEAD_DIM)
        v_g = gather(vs_blk, idx).reshape(b, N_KV_HEADS, Q_BLOCK, n_sel * SEL_BLOCK, HEAD_DIM)
        kpos = (idx[..., None] * SEL_BLOCK + jnp.arange(SEL_BLOCK)).reshape(b, N_KV_HEADS, Q_BLOCK, n_sel * SEL_BLOCK)
        s_s = jnp.einsum('bhgqd,bhqkd->bhgqk', qb, k_g) * scale
        p_s = masked_softmax(s_s, (kpos <= t[:, None])[:, :, None])
        o_s = jnp.einsum('bhgqk,bhqkd->bhgqd', p_s.astype(v_g.dtype), v_g)
        k_w = lax.dynamic_slice_in_dim(kw_pad, q0, Q_BLOCK + WINDOW, axis=2)
        v_w = lax.dynamic_slice_in_dim(vw_pad, q0, Q_BLOCK + WINDOW, axis=2)
        wpos = q0 - WINDOW + jnp.arange(Q_BLOCK + WINDOW)
        wmask = (wpos[None, :] >= 0) & (wpos[None, :] <= t[:, None]) & (wpos[None, :] > t[:, None] - WINDOW)
        s_w = jnp.einsum('bhgqd,bhkd->bhgqk', qb, k_w) * scale
        p_w = masked_softmax(s_w, wmask)
        o_w = jnp.einsum('bhgqk,bhkd->bhgqd', p_w.astype(v_w.dtype), v_w)
        o = gb[..., 0:1] * o_c + gb[..., 1:2] * o_s + gb[..., 2:3] * o_w
        return o.astype(q.dtype)

    out = lax.map(query_block, jnp.arange(n_qb))
    return out.transpose(1, 0, 4, 2, 3, 5).reshape(b, s, ATTN_WIDTH)


def setup_inputs(seed: int = 0) -> dict:
    key = jax.random.key(seed)
    ks = jax.random.split(key, 22)
    f32 = jnp.float32
    L = DEPTH

    def nrm(k, shape, fan_in):
        return jax.random.normal(k, shape, f32) * fan_in ** -0.5

    def gain(k, shape):
        return 1.0 + 0.02 * jax.random.normal(k, shape, f32)

    return {
        'x': jax.random.normal(ks[0], (BATCH, SEQ, D_MODEL), f32),
        'p': jax.random.normal(ks[1], (DEPTH, BATCH, SEQ, PLE_DIM), f32),
        'g_mix': gain(ks[2], (L, D_MODEL)),
        'w_in': nrm(ks[3], (L, D_MODEL, IN_COLS), D_MODEL),
        'w_conv_mix': nrm(ks[4], (L, CONV_K, CONV_WIDTH), CONV_K),
        'cmp_pe_k': 0.1 * jax.random.normal(ks[5], (L, CMP_LEN, HEAD_DIM), f32),
        'cmp_w1_k': nrm(ks[6], (L, CMP_LEN * HEAD_DIM, HEAD_DIM), CMP_LEN * HEAD_DIM),
        'cmp_w2_k': nrm(ks[7], (L, HEAD_DIM, HEAD_DIM), HEAD_DIM),
        'cmp_pe_v': 0.1 * jax.random.normal(ks[8], (L, CMP_LEN, HEAD_DIM), f32),
        'cmp_w1_v': nrm(ks[9], (L, CMP_LEN * HEAD_DIM, HEAD_DIM), CMP_LEN * HEAD_DIM),
        'cmp_w2_v': nrm(ks[10], (L, HEAD_DIM, HEAD_DIM), HEAD_DIM),
        'g_gn_conv': gain(ks[11], (L, CONV_WIDTH)),
        'g_gn_attn': gain(ks[12], (L, ATTN_WIDTH)),
        'w_out': nrm(ks[13], (L, MIX_WIDTH, D_MODEL), MIX_WIDTH),
        'g_ffn': gain(ks[14], (L, D_MODEL)),
        'w_up': nrm(ks[15], (L, D_MODEL, 2 * D_FF), D_MODEL),
        'w_ffn_conv': nrm(ks[16], (L, FFN_K, 2 * D_FF), FFN_K),
        'w_down': nrm(ks[17], (L, D_FF, D_MODEL), D_FF),
        'g_ple': gain(ks[18], (L, D_MODEL)),
        'w_ple_gate': nrm(ks[19], (L, D_MODEL, D_MODEL), D_MODEL),
        'w_ple_proj': nrm(ks[20], (L, PLE_DIM, D_MODEL), PLE_DIM),
        'g_final': gain(ks[21], (D_MODEL,)),
    }


def reference(x, p, g_mix, w_in, w_conv_mix, cmp_pe_k, cmp_w1_k, cmp_w2_k, cmp_pe_v, cmp_w1_v,
              cmp_w2_v, g_gn_conv, g_gn_attn, w_out, g_ffn, w_up, w_ffn_conv, w_down, g_ple,
              w_ple_gate, w_ple_proj, g_final):
    b, s, _ = x.shape
    pos = jnp.arange(s, dtype=jnp.int32)
    cos, sin = rope_tables(pos)
    cos_h, sin_h = cos[:, None, :], sin[:, None, :]
    n_cmp = (s - CMP_LEN) // CMP_STRIDE + 1
    n_blk = s // SEL_BLOCK
    cmp_start = jnp.arange(n_cmp) * CMP_STRIDE
    cmp_idx = cmp_start[:, None] + jnp.arange(CMP_LEN)[None, :]
    cmp_end = cmp_start + CMP_LEN - 1
    cos_c, sin_c = rope_tables(cmp_end)
    blk_start = jnp.arange(n_blk) * SEL_BLOCK
    overlap = (jnp.clip(jnp.minimum(cmp_start[:, None] + CMP_LEN, blk_start[None, :] + SEL_BLOCK)
                        - jnp.maximum(cmp_start[:, None], blk_start[None, :]), 0, None)
               .astype(jnp.float32) / CMP_LEN)
    sizes = [CONV_WIDTH] * 3 + [ATTN_WIDTH] + [KV_WIDTH] * (2 * N_BRANCH) + [N_BRANCH * N_HEADS]
    split_at = np.cumsum(sizes)[:-1].tolist()

    def kv_heads(t):
        return t.reshape(b, s, N_KV_HEADS, HEAD_DIM)

    h = x
    for i in range(DEPTH):
        n1 = rms_norm(h, g_mix[i])
        proj = n1 @ w_in[i]
        (x_in, b_gate, c_gate, q, k_cmp, v_cmp, k_slc, v_slc, k_win, v_win,
         gate_logits) = jnp.split(proj, split_at, axis=-1)
        y_conv = short_conv_mixer(x_in, b_gate, c_gate, w_conv_mix[i])

        q = apply_rope(q.reshape(b, s, N_HEADS, HEAD_DIM), cos_h, sin_h)
        ks_r = apply_rope(kv_heads(k_slc), cos_h, sin_h).transpose(0, 2, 1, 3)
        kw_r = apply_rope(kv_heads(k_win), cos_h, sin_h).transpose(0, 2, 1, 3)
        vs_t = kv_heads(v_slc).transpose(0, 2, 1, 3)
        vw_t = kv_heads(v_win).transpose(0, 2, 1, 3)
        kc = compress_kv(kv_heads(k_cmp).transpose(0, 2, 1, 3), cmp_idx, cmp_pe_k[i], cmp_w1_k[i], cmp_w2_k[i])
        kc = apply_rope(kc, cos_c, sin_c)
        vc = compress_kv(kv_heads(v_cmp).transpose(0, 2, 1, 3), cmp_idx, cmp_pe_v[i], cmp_w1_v[i], cmp_w2_v[i])
        gates = jax.nn.sigmoid(gate_logits).reshape(b, s, N_HEADS, N_BRANCH)
        y_attn = native_sparse_attention(q, kc, vc, ks_r, vs_t, kw_r, vw_t, gates, cmp_end, overlap)

        mixed = jnp.concatenate([rms_norm(y_conv, g_gn_conv[i]), rms_norm(y_attn, g_gn_attn[i])], axis=-1)
        h = h + mixed @ w_out[i]

        u = causal_dwconv(rms_norm(h, g_ffn[i]) @ w_up[i], w_ffn_conv[i])
        u_gate, u_val = jnp.split(u, 2, axis=-1)
        h = h + (jax.nn.silu(u_gate) * u_val) @ w_down[i]

        ple_gate = jax.nn.sigmoid(rms_norm(h, g_ple[i]) @ w_ple_gate[i])
        h = h + ple_gate * (p[i] @ w_ple_proj[i])

    return rms_norm(h, g_final)
```

```python
import functools

import numpy as np
import jax
import jax.numpy as jnp
from jax import lax
from jax.experimental import pallas as pl
from jax.experimental.pallas import tpu as pltpu

F32 = jnp.float32
BF16 = jnp.bfloat16

D_MODEL = 1024
PLE_DIM = 256
CONV_WIDTH = 512
N_HEADS = 8
N_KV_HEADS = 2
HEAD_DIM = 64
HALF_DIM = HEAD_DIM // 2
GROUP_SIZE = N_HEADS // N_KV_HEADS
ATTN_WIDTH = N_HEADS * HEAD_DIM
KV_WIDTH = N_KV_HEADS * HEAD_DIM
N_BRANCH = 3
CMP_LEN = 32
CMP_STRIDE = 16
SEL_BLOCK = 64
N_SEL = 8
WINDOW = 512
Q_BLOCK = 128
D_FF = 2816
ROPE_THETA = 10000.0
EPS = 1e-6
NEG = -1e30
SEL_FORCE = 1e4

LANES = 128
BF16_ROWS = 16
VMEM_LIMIT = 56 * 1024 * 1024

COL_XIN = 0
COL_B = CONV_WIDTH
COL_C = 2 * CONV_WIDTH
COL_Q = 3 * CONV_WIDTH
COL_KVC = COL_Q + ATTN_WIDTH
COL_KV4 = COL_KVC + 2 * KV_WIDTH
COL_GATE = COL_KV4 + 4 * LANES
IN_COLS_PAD = COL_GATE + N_KV_HEADS * LANES

TM_IN = 512
TM_OUT = 512
TM_FFN = 1024
TF_FFN = 256
TM_PLE = 512
CMP_ROWS = 512
SEL_CHUNK = 512


def _dot(a, b):
    return jnp.dot(a, b, preferred_element_type=F32)


def _dot_nt(a, b):
    return lax.dot_general(a, b, (((1,), (1,)), ((), ())), preferred_element_type=F32)


def _sigmoid(x):
    return 1.0 / (1.0 + jnp.exp(-x))


def _rms(x, g):
    return x * lax.rsqrt(jnp.mean(x * x, axis=-1, keepdims=True) + EPS) * g


def _rope_slab(y, cos, sin_signed, first_half):
    fwd = pltpu.roll(y, HALF_DIM, 1)
    bwd = pltpu.roll(y, LANES - HALF_DIM, 1)
    return y * cos + jnp.where(first_half, bwd, fwd) * sin_signed


def _shift_rows(a, halo, seq_start):
    rows = a.shape[0]
    halo = jnp.where(seq_start, 0.0, halo)
    h1 = halo[BF16_ROWS - 1:BF16_ROWS, :]
    h2 = halo[BF16_ROWS - 2:BF16_ROWS - 1, :]
    row = lax.broadcasted_iota(jnp.int32, a.shape, 0)
    s1 = jnp.where(row == 0, h1, pltpu.roll(a, 1, 0))
    s2 = jnp.where(row == 0, h2, jnp.where(row == 1, h1, pltpu.roll(a, 2, 0)))
    del rows
    return s1, s2


def _inproj_kernel(x_ref, g_ref, w_ref, cos_ref, sin_ref,
                   cx_ref, bg_ref, q_ref, kvc_ref, kv4_ref, gate_ref):
    n = _rms(x_ref[...], g_ref[...]).astype(BF16)

    def proj(lo, hi):
        return _dot(n, w_ref[:, lo:hi])

    x_in = proj(COL_XIN, COL_B)
    bg_ref[...] = proj(COL_B, COL_C).astype(BF16)
    cx_ref[...] = (proj(COL_C, COL_Q) * x_in).astype(BF16)

    cos = cos_ref[...]
    sin = sin_ref[...]
    lane = lax.broadcasted_iota(jnp.int32, cos.shape, 1)
    first_half = (lane % HEAD_DIM) < HALF_DIM
    scale = HEAD_DIM ** -0.5
    for s in range(ATTN_WIDTH // LANES):
        y = proj(COL_Q + s * LANES, COL_Q + (s + 1) * LANES)
        q_ref[:, s * LANES:(s + 1) * LANES] = (_rope_slab(y, cos, sin, first_half) * scale).astype(BF16)

    kvc_ref[...] = proj(COL_KVC, COL_KV4).astype(BF16)

    is_k = lane < HEAD_DIM
    cos_kv = jnp.where(is_k, cos, 1.0)
    sin_kv = jnp.where(is_k, sin, 0.0)
    for s in range(4):
        y = proj(COL_KV4 + s * LANES, COL_KV4 + (s + 1) * LANES)
        kv4_ref[:, s * LANES:(s + 1) * LANES] = _rope_slab(y, cos_kv, sin_kv, first_half).astype(BF16)

    gate_ref[...] = _sigmoid(proj(COL_GATE, IN_COLS_PAD))


def _inproj(x2, g_mix, w_perm, cos_q, sin_q, seq):
    t = x2.shape[0]
    tm = TM_IN
    n_seq_tiles = seq // tm
    row = lambda i: (i, 0)
    fixed = lambda i: (0, 0)
    tab = lambda i: (i % n_seq_tiles, 0)
    return pl.pallas_call(
        _inproj_kernel,
        grid=(t // tm,),
        in_specs=[
            pl.BlockSpec((tm, D_MODEL), row),
            pl.BlockSpec((1, D_MODEL), fixed),
            pl.BlockSpec((D_MODEL, IN_COLS_PAD), fixed),
            pl.BlockSpec((tm, LANES), tab),
            pl.BlockSpec((tm, LANES), tab),
        ],
        out_specs=[
            pl.BlockSpec((tm, CONV_WIDTH), row),
            pl.BlockSpec((tm, CONV_WIDTH), row),
            pl.BlockSpec((tm, ATTN_WIDTH), row),
            pl.BlockSpec((tm, 2 * KV_WIDTH), row),
            pl.BlockSpec((tm, 4 * LANES), row),
            pl.BlockSpec((tm, N_KV_HEADS * LANES), row),
        ],
        out_shape=[
            jax.ShapeDtypeStruct((t, CONV_WIDTH), BF16),
            jax.ShapeDtypeStruct((t, CONV_WIDTH), BF16),
            jax.ShapeDtypeStruct((t, ATTN_WIDTH), BF16),
            jax.ShapeDtypeStruct((t, 2 * KV_WIDTH), BF16),
            jax.ShapeDtypeStruct((t, 4 * LANES), BF16),
            jax.ShapeDtypeStruct((t, N_KV_HEADS * LANES), F32),
        ],
        compiler_params=pltpu.CompilerParams(
            dimension_semantics=("arbitrary",), vmem_limit_bytes=VMEM_LIMIT),
        name="inproj",
    )(x2, g_mix, w_perm, cos_q, sin_q)


def _compress_kernel(r_ref, w1_ref, pe_ref, w2_ref, cos_ref, sin_ref, out_ref):
    width = 2 * KV_WIDTH
    p = _dot(r_ref[...], w1_ref[...])
    pb = _dot(pe_ref[...], w1_ref[...])
    bias = pb[0:1, :width] + pb[1:2, width:]
    rows = p.shape[0]
    nxt = pltpu.roll(p[:, width:], rows - 1, 0)
    pre = p[:, :width] + nxt + bias
    hid = (pre * _sigmoid(pre)).astype(BF16)
    y = _dot(hid, w2_ref[...])
    cos = cos_ref[...]
    sin = sin_ref[...]
    lane = lax.broadcasted_iota(jnp.int32, (rows, LANES), 1)
    first_half = (lane % HEAD_DIM) < HALF_DIM
    for s in range(width // LANES):
        sl = slice(s * LANES, (s + 1) * LANES)
        out_ref[:, sl] = _rope_slab(y[:, sl], cos[:, sl], sin[:, sl], first_half).astype(BF16)


def _compress(r, w1_big, pe_big, w2_big, cos_c, sin_c):
    rows = r.shape[0]
    width = 2 * KV_WIDTH
    row = lambda i: (i, 0)
    fixed = lambda i: (0, 0)
    return pl.pallas_call(
        _compress_kernel,
        grid=(rows // CMP_ROWS,),
        in_specs=[
            pl.BlockSpec((CMP_ROWS, r.shape[1]), row),
            pl.BlockSpec(w1_big.shape, fixed),
            pl.BlockSpec(pe_big.shape, fixed),
            pl.BlockSpec(w2_big.shape, fixed),
            pl.BlockSpec((CMP_ROWS, width), fixed),
            pl.BlockSpec((CMP_ROWS, width), fixed),
        ],
        out_specs=pl.BlockSpec((CMP_ROWS, width), row),
        out_shape=jax.ShapeDtypeStruct((rows, width), BF16),
        compiler_params=pltpu.CompilerParams(
            dimension_semantics=("arbitrary",), vmem_limit_bytes=VMEM_LIMIT),
        name="compress",
    )(r, w1_big, pe_big, w2_big, cos_c, sin_c)


def _masked_softmax(s, mask):
    s = jnp.where(mask, s, NEG)
    m = jnp.max(s, axis=-1, keepdims=True)
    e = jnp.where(mask, jnp.exp(s - m), 0.0)
    return e / jnp.maximum(jnp.sum(e, axis=-1, keepdims=True), 1e-30)


def _nsa_kernel(q_ref, gate_ref, cmp_ref, slc_ref, win_ref, ovl_ref, exp_ref, o_ref):
    i = pl.program_id(2)
    q0 = i * Q_BLOCK
    g_rows = GROUP_SIZE * Q_BLOCK

    qf = q_ref[...].astype(F32)
    lane_q = lax.broadcasted_iota(jnp.int32, (Q_BLOCK, LANES), 1)
    parts = []
    for g in range(GROUP_SIZE):
        slab = qf[:, (g // 2) * LANES:(g // 2 + 1) * LANES]
        if g % 2 == 1:
            slab = pltpu.roll(slab, HEAD_DIM, 1)
        parts.append(jnp.where(lane_q < HEAD_DIM, slab, 0.0).astype(BF16))
    qs = jnp.concatenate(parts, axis=0)

    def rows_t(ncols):
        r = lax.broadcasted_iota(jnp.int32, (g_rows, ncols), 0)
        return q0 + r % Q_BLOCK

    def cols(ncols):
        return lax.broadcasted_iota(jnp.int32, (g_rows, ncols), 1)

    kcv = cmp_ref[...]
    n_cmp = kcv.shape[0]
    s_c = _dot_nt(qs, kcv)
    mask_c = cols(n_cmp) * CMP_STRIDE + (CMP_LEN - 1) <= rows_t(n_cmp)
    p_c = _masked_softmax(s_c, mask_c)
    o_c = _dot(p_c.astype(BF16), kcv)

    psum = p_c[0:Q_BLOCK]
    for g in range(1, GROUP_SIZE):
        psum = psum + p_c[g * Q_BLOCK:(g + 1) * Q_BLOCK]
    p_hi = psum.astype(BF16)
    p_lo = (psum - p_hi.astype(F32)).astype(BF16)
    ovl = ovl_ref[...]
    imp = _dot_nt(ovl, p_hi) + _dot_nt(ovl, p_lo)
    n_blk = imp.shape[0]
    blk = lax.broadcasted_iota(jnp.int32, (n_blk, Q_BLOCK), 0)
    t_q = q0 + lax.broadcasted_iota(jnp.int32, (n_blk, Q_BLOCK), 1)
    cur = t_q // SEL_BLOCK
    valid = blk * SEL_BLOCK <= t_q
    forced = (blk == 0) | (blk == cur) | (blk == cur - 1)
    score = jnp.where(valid, imp + jnp.where(forced, SEL_FORCE, 0.0), NEG)
    rank = jnp.zeros((n_blk, Q_BLOCK), F32)
    for m in range(n_blk):
        other = score[m:m + 1, :]
        beats = (other > score) | ((other == score) & (blk > m))
        rank = rank + jnp.where(beats, 1.0, 0.0)
    sel_t = jnp.where(rank < float(min(N_SEL, n_blk)), 1.0, 0.0)
    sel_t = jnp.concatenate([sel_t, jnp.zeros((LANES - n_blk, Q_BLOCK), F32)], axis=0)
    sel = sel_t.T.astype(BF16)

    def sel_step(c, carry):
        m_i, l_i, acc = carry
        k0 = pl.multiple_of(c * SEL_CHUNK, SEL_CHUNK)
        kv = slc_ref[pl.ds(k0, SEL_CHUNK), :]
        s = _dot_nt(qs, kv)
        in_sel = _dot(sel, exp_ref[:, pl.ds(k0, SEL_CHUNK)])
        in_sel = jnp.concatenate([in_sel] * GROUP_SIZE, axis=0)
        mask = (in_sel > 0.5) & (k0 + cols(SEL_CHUNK) <= rows_t(SEL_CHUNK))
        s = jnp.where(mask, s, NEG)
        m_new = jnp.maximum(m_i, jnp.max(s, axis=-1, keepdims=True))
        alpha = jnp.exp(m_i - m_new)
        e = jnp.where(mask, jnp.exp(s - m_new), 0.0)
        l_new = alpha * l_i + jnp.sum(e, axis=-1, keepdims=True)
        acc = alpha * acc + _dot(e.astype(BF16), kv)
        return m_new, l_new, acc

    n_chunks = (q0 + Q_BLOCK + SEL_CHUNK - 1) // SEL_CHUNK
    init = (jnp.full((g_rows, 1), NEG, F32), jnp.zeros((g_rows, 1), F32), jnp.zeros((g_rows, LANES), F32))
    _, l_s, acc_s = lax.fori_loop(0, n_chunks, sel_step, init)
    o_s = acc_s / jnp.maximum(l_s, 1e-30)

    n_win = Q_BLOCK + WINDOW
    w0 = pl.multiple_of(jnp.maximum(q0 - WINDOW, 0), Q_BLOCK)
    kvw = win_ref[pl.ds(w0, n_win), :]
    s_w = _dot_nt(qs, kvw)
    wpos = w0 + cols(n_win)
    t_w = rows_t(n_win)
    p_w = _masked_softmax(s_w, (wpos <= t_w) & (wpos > t_w - WINDOW))
    o_w = _dot(p_w.astype(BF16), kvw)

    gate = gate_ref[...]

    def gate_col(r):
        return jnp.concatenate(
            [gate[:, g * N_BRANCH + r:g * N_BRANCH + r + 1] for g in range(GROUP_SIZE)], axis=0)

    o = gate_col(0) * o_c + gate_col(1) * o_s + gate_col(2) * o_w
    for j in range(GROUP_SIZE // 2):
        even = pltpu.roll(o[(2 * j) * Q_BLOCK:(2 * j + 1) * Q_BLOCK], HEAD_DIM, 1)
        odd = o[(2 * j + 1) * Q_BLOCK:(2 * j + 2) * Q_BLOCK]
        o_ref[:, j * LANES:(j + 1) * LANES] = jnp.where(lane_q < HEAD_DIM, even, odd).astype(BF16)


def _nsa(q, gates, kcv, kv4, ovl_t, expand, batch, seq):
    n_qb = seq // Q_BLOCK
    n_cmp_pad = kcv.shape[0] // batch
    grp = GROUP_SIZE * HEAD_DIM
    return pl.pallas_call(
        _nsa_kernel,
        grid=(batch, N_KV_HEADS, n_qb),
        in_specs=[
            pl.BlockSpec((Q_BLOCK, grp), lambda b, h, i: (b * n_qb + i, h)),
            pl.BlockSpec((Q_BLOCK, LANES), lambda b, h, i: (b * n_qb + i, h)),
            pl.BlockSpec((n_cmp_pad, LANES), lambda b, h, i: (b, h)),
            pl.BlockSpec((seq, LANES), lambda b, h, i: (b, h)),
            pl.BlockSpec((seq, LANES), lambda b, h, i: (b, N_KV_HEADS + h)),
            pl.BlockSpec(ovl_t.shape, lambda b, h, i: (0, 0)),
            pl.BlockSpec(expand.shape, lambda b, h, i: (0, 0)),
        ],
        out_specs=pl.BlockSpec((Q_BLOCK, grp), lambda b, h, i: (b * n_qb + i, h)),
        out_shape=jax.ShapeDtypeStruct((batch * seq, ATTN_WIDTH), BF16),
        compiler_params=pltpu.CompilerParams(
            dimension_semantics=("arbitrary", "arbitrary", "arbitrary"), vmem_limit_bytes=VMEM_LIMIT),
        name="nsa",
    )(q, gates, kcv, kv4, kv4, ovl_t, expand)


def _outproj_kernel(cx_ref, cxh_ref, bg_ref, y_ref, x_ref, wc_ref, gc_ref, ga_ref, w_ref, gf_ref,
                    h_ref, n2_ref, *, tiles_per_seq):
    seq_start = pl.program_id(0) % tiles_per_seq == 0
    cx = cx_ref[...].astype(F32)
    s1, s2 = _shift_rows(cx, cxh_ref[...].astype(F32), seq_start)
    wc = wc_ref[...]
    conv = wc[0:1] * s2 + wc[1:2] * s1 + wc[2:3] * cx
    y_conv = bg_ref[...].astype(F32) * conv
    mixed = jnp.concatenate(
        [_rms(y_conv, gc_ref[...]).astype(BF16), _rms(y_ref[...].astype(F32), ga_ref[...]).astype(BF16)], axis=1)
    h = x_ref[...] + _dot(mixed, w_ref[...])
    h_ref[...] = h
    n2_ref[...] = _rms(h, gf_ref[...]).astype(BF16)


def _outproj(cx, bg, y_attn, x2, w_conv, g_conv, g_attn, w_out, g_ffn, seq):
    t = x2.shape[0]
    tm = TM_OUT
    row = lambda i: (i, 0)
    fixed = lambda i: (0, 0)
    halo = lambda i: (jnp.maximum(i * (tm // BF16_ROWS) - 1, 0), 0)
    return pl.pallas_call(
        functools.partial(_outproj_kernel, tiles_per_seq=seq // tm),
        grid=(t // tm,),
        in_specs=[
            pl.BlockSpec((tm, CONV_WIDTH), row),
            pl.BlockSpec((BF16_ROWS, CONV_WIDTH), halo),
            pl.BlockSpec((tm, CONV_WIDTH), row),
            pl.BlockSpec((tm, ATTN_WIDTH), row),
            pl.BlockSpec((tm, D_MODEL), row),
            pl.BlockSpec(w_conv.shape, fixed),
            pl.BlockSpec((1, CONV_WIDTH), fixed),
            pl.BlockSpec((1, ATTN_WIDTH), fixed),
            pl.BlockSpec(w_out.shape, fixed),
            pl.BlockSpec((1, D_MODEL), fixed),
        ],
        out_specs=[pl.BlockSpec((tm, D_MODEL), row), pl.BlockSpec((tm, D_MODEL), row)],
        out_shape=[jax.ShapeDtypeStruct((t, D_MODEL), F32), jax.ShapeDtypeStruct((t, D_MODEL), BF16)],
        compiler_params=pltpu.CompilerParams(
            dimension_semantics=("arbitrary",), vmem_limit_bytes=VMEM_LIMIT),
        name="outproj",
    )(cx, cx, bg, y_attn, x2, w_conv, g_conv, g_attn, w_out, g_ffn)


def _ffn_kernel(n_ref, nh_ref, h_ref, wg_ref, wv_ref, cg_ref, cv_ref, wd_ref, out_ref, *, tiles_per_seq):
    seq_start = pl.program_id(0) % tiles_per_seq == 0
    j = pl.program_id(1)

    @pl.when(j == 0)
    def _():
        out_ref[...] = h_ref[...]

    n = n_ref[...]
    nh = nh_ref[...]

    def conv_up(w_ref, c_ref):
        w = w_ref[...]
        a = _dot(n, w)
        s1, s2 = _shift_rows(a, _dot(nh, w), seq_start)
        c = c_ref[...]
        return c[0:1] * s2 + c[1:2] * s1 + c[2:3] * a

    u_gate = conv_up(wg_ref, cg_ref)
    u_val = conv_up(wv_ref, cv_ref)
    act = (u_gate * _sigmoid(u_gate) * u_val).astype(BF16)
    out_ref[...] += _dot(act, wd_ref[...])


def _ffn(n2, h1, w_up, w_conv, w_down, seq):
    t = n2.shape[0]
    tm, tf = TM_FFN, TF_FFN
    n_f = D_FF // tf
    row = lambda i, j: (i, 0)
    halo = lambda i, j: (jnp.maximum(i * (tm // BF16_ROWS) - 1, 0), 0)
    return pl.pallas_call(
        functools.partial(_ffn_kernel, tiles_per_seq=seq // tm),
        grid=(t // tm, n_f),
        in_specs=[
            pl.BlockSpec((tm, D_MODEL), row),
            pl.BlockSpec((BF16_ROWS, D_MODEL), halo),
            pl.BlockSpec((tm, D_MODEL), row),
            pl.BlockSpec((D_MODEL, tf), lambda i, j: (0, j)),
            pl.BlockSpec((D_MODEL, tf), lambda i, j: (0, n_f + j)),
            pl.BlockSpec((w_conv.shape[0], tf), lambda i, j: (0, j)),
            pl.BlockSpec((w_conv.shape[0], tf), lambda i, j: (0, n_f + j)),
            pl.BlockSpec((tf, D_MODEL), lambda i, j: (j, 0)),
        ],
        out_specs=pl.BlockSpec((tm, D_MODEL), row),
        out_shape=jax.ShapeDtypeStruct((t, D_MODEL), F32),
        compiler_params=pltpu.CompilerParams(
            dimension_semantics=("arbitrary", "arbitrary"), vmem_limit_bytes=VMEM_LIMIT),
        name="ffn",
    )(n2, n2, h1, w_up, w_up, w_conv, w_conv, w_down)


def _ple_kernel(h_ref, p_ref, gp_ref, wg_ref, wp_ref, gfin_ref, out_ref):
    h = h_ref[...]
    gate = _sigmoid(_dot(_rms(h, gp_ref[...]).astype(BF16), wg_ref[...]))
    h = h + gate * _dot(p_ref[...].astype(BF16), wp_ref[...])
    out_ref[...] = _rms(h, gfin_ref[...])


def _ple(h2, p2, g_ple, w_gate, w_proj, g_final):
    t = h2.shape[0]
    tm = TM_PLE
    row = lambda i: (i, 0)
    fixed = lambda i: (0, 0)
    return pl.pallas_call(
        _ple_kernel,
        grid=(t // tm,),
        in_specs=[
            pl.BlockSpec((tm, D_MODEL), row),
            pl.BlockSpec((tm, PLE_DIM), row),
            pl.BlockSpec((1, D_MODEL), fixed),
            pl.BlockSpec(w_gate.shape, fixed),
            pl.BlockSpec(w_proj.shape, fixed),
            pl.BlockSpec((1, D_MODEL), fixed),
        ],
        out_specs=pl.BlockSpec((tm, D_MODEL), row),
        out_shape=jax.ShapeDtypeStruct((t, D_MODEL), F32),
        compiler_params=pltpu.CompilerParams(
            dimension_semantics=("arbitrary",), vmem_limit_bytes=VMEM_LIMIT),
        name="ple",
    )(h2, p2, g_ple, w_gate, w_proj, g_final)


def _in_col_perm():
    in_cols = 3 * CONV_WIDTH + ATTN_WIDTH + 2 * N_BRANCH * KV_WIDTH + N_BRANCH * N_HEADS
    base_kv = 3 * CONV_WIDTH + ATTN_WIDTH
    k_slc, v_slc, k_win, v_win = (base_kv + n * KV_WIDTH for n in (2, 3, 4, 5))
    base_gate = base_kv + 2 * N_BRANCH * KV_WIDTH
    idx = list(range(COL_KV4))
    for k_col, v_col in ((k_slc, v_slc), (k_win, v_win)):
        for h in range(N_KV_HEADS):
            idx += list(range(k_col + h * HEAD_DIM, k_col + (h + 1) * HEAD_DIM))
            idx += list(range(v_col + h * HEAD_DIM, v_col + (h + 1) * HEAD_DIM))
    per_kv = GROUP_SIZE * N_BRANCH
    for h in range(N_KV_HEADS):
        idx += list(range(base_gate + h * per_kv, base_gate + (h + 1) * per_kv))
        idx += [in_cols] * (LANES - per_kv)
    assert len(idx) == IN_COLS_PAD
    return np.asarray(idx, np.int32), in_cols


def _rope_tables(pos):
    inv = ROPE_THETA ** (-jnp.arange(0, HEAD_DIM, 2, dtype=F32) / HEAD_DIM)
    ang = pos.astype(F32)[:, None] * inv[None, :]
    cos = jnp.concatenate([jnp.cos(ang), jnp.cos(ang)], axis=-1)
    sin = jnp.concatenate([-jnp.sin(ang), jnp.sin(ang)], axis=-1)
    return cos, sin


def _compress_weights(w1_k, w1_v, w2_k, w2_v, pe_k, pe_v):
    n_str = 2 * N_KV_HEADS
    half = CMP_LEN // 2
    w1 = jnp.stack([w1_k, w1_k, w1_v, w1_v]).reshape(n_str, 2, half, HEAD_DIM, HEAD_DIM)
    eye = jnp.eye(n_str, dtype=F32)
    w1_big = jnp.einsum('chrde,cx->rcdhxe', w1, eye).reshape(half * n_str * HEAD_DIM, 2 * n_str * HEAD_DIM)
    pe = jnp.stack([pe_k, pe_k, pe_v, pe_v]).reshape(n_str, 2, half, HEAD_DIM)
    pe_big = jnp.transpose(pe, (1, 2, 0, 3)).reshape(2, half * n_str * HEAD_DIM)
    pe_big = jnp.concatenate([pe_big, jnp.zeros((6, pe_big.shape[1]), F32)], axis=0)
    w2 = jnp.stack([w2_k, w2_k, w2_v, w2_v])
    out_pos = np.asarray([0, 2, 1, 3])
    place = jnp.zeros((n_str, n_str), F32).at[np.arange(n_str), out_pos].set(1.0)
    w2_big = jnp.einsum('cde,cx->cdxe', w2, place).reshape(n_str * HEAD_DIM, n_str * HEAD_DIM)
    return w1_big.astype(BF16), pe_big.astype(BF16), w2_big.astype(BF16)


def kernel(x, p, g_mix, w_in, w_conv_mix, cmp_pe_k, cmp_w1_k, cmp_w2_k, cmp_pe_v, cmp_w1_v, cmp_w2_v,
           g_gn_conv, g_gn_attn, w_out, g_ffn, w_up, w_ffn_conv, w_down, g_ple, w_ple_gate, w_ple_proj,
           g_final):
    batch, seq, _ = x.shape
    depth = w_in.shape[0]
    t = batch * seq
    n_cmp_pad = seq // CMP_STRIDE
    n_blk = seq // SEL_BLOCK

    cos, sin = _rope_tables(jnp.arange(seq, dtype=jnp.int32))
    cos_q = jnp.concatenate([cos, cos], axis=-1)
    sin_q = jnp.concatenate([sin, sin], axis=-1)
    cmp_start = jnp.arange(n_cmp_pad) * CMP_STRIDE
    cos_e, sin_e = _rope_tables(cmp_start + CMP_LEN - 1)
    ones = jnp.ones_like(cos_e)
    zeros = jnp.zeros_like(sin_e)
    reps = CMP_ROWS // n_cmp_pad
    cos_c = jnp.tile(jnp.concatenate([cos_e, ones, cos_e, ones], axis=-1), (reps, 1))
    sin_c = jnp.tile(jnp.concatenate([sin_e, zeros, sin_e, zeros], axis=-1), (reps, 1))
    blk_start = jnp.arange(n_blk) * SEL_BLOCK
    n_cmp = (seq - CMP_LEN) // CMP_STRIDE + 1
    ovl = (jnp.clip(jnp.minimum(cmp_start[:, None] + CMP_LEN, blk_start[None, :] + SEL_BLOCK)
                    - jnp.maximum(cmp_start[:, None], blk_start[None, :]), 0, None).astype(F32) / CMP_LEN)
    ovl = jnp.where(jnp.arange(n_cmp_pad)[:, None] < n_cmp, ovl, 0.0)
    ovl_t = ovl.T.astype(BF16)
    expand = (jnp.arange(seq)[None, :] // SEL_BLOCK == jnp.arange(LANES)[:, None]).astype(BF16)

    perm, in_cols = _in_col_perm()
    pad_rows = lambda w: jnp.concatenate([w, jnp.zeros((8 - w.shape[0], w.shape[1]), w.dtype)], axis=0)

    h = x.reshape(t, D_MODEL)
    for i in range(depth):
        w_ext = jnp.concatenate([w_in[i], jnp.zeros((D_MODEL, 1), F32)], axis=1)
        w_perm = jnp.take(w_ext, perm, axis=1).astype(BF16)
        cx, bg, q, kvc, kv4, gates = _inproj(h, g_mix[i][None, :], w_perm, cos_q, sin_q, seq)

        w1_big, pe_big, w2_big = _compress_weights(
            cmp_w1_k[i], cmp_w1_v[i], cmp_w2_k[i], cmp_w2_v[i], cmp_pe_k[i], cmp_pe_v[i])
        r = kvc.reshape(t // CMP_STRIDE, CMP_STRIDE * 2 * KV_WIDTH)
        kcv = _compress(r, w1_big, pe_big, w2_big, cos_c, sin_c)

        y_attn = _nsa(q, gates, kcv, kv4, ovl_t, expand, batch, seq)

        h1, n2 = _outproj(cx, bg, y_attn, h, pad_rows(w_conv_mix[i]), g_gn_conv[i][None, :],
                          g_gn_attn[i][None, :], w_out[i].astype(BF16), g_ffn[i][None, :], seq)

        h2 = _ffn(n2, h1, w_up[i].astype(BF16), pad_rows(w_ffn_conv[i]), w_down[i].astype(BF16), seq)

        assert depth == 1
        h = _ple(h2, p[i].reshape(t, PLE_DIM), g_ple[i][None, :], w_ple_gate[i].astype(BF16),
                 w_ple_proj[i].astype(BF16), g_final[None, :])
    return h.reshape(batch, seq, D_MODEL)
```

```python
import functools

import numpy as np
import jax
import jax.numpy as jnp
from jax import lax
from jax.experimental import pallas as pl
from jax.experimental.pallas import tpu as pltpu

F32 = jnp.float32
BF16 = jnp.bfloat16

D_MODEL = 1024
PLE_DIM = 256
CONV_WIDTH = 512
N_HEADS = 8
N_KV_HEADS = 2
HEAD_DIM = 64
HALF_DIM = HEAD_DIM // 2
GROUP_SIZE = N_HEADS // N_KV_HEADS
ATTN_WIDTH = N_HEADS * HEAD_DIM
KV_WIDTH = N_KV_HEADS * HEAD_DIM
N_BRANCH = 3
CMP_LEN = 32
CMP_STRIDE = 16
SEL_BLOCK = 64
N_SEL = 8
WINDOW = 512
Q_BLOCK = 128
D_FF = 2816
ROPE_THETA = 10000.0
EPS = 1e-6
NEG = -1e30
SEL_FORCE = 1e4

LANES = 128
BF16_ROWS = 16
VMEM_LIMIT = 56 * 1024 * 1024

COL_XIN = 0
COL_B = CONV_WIDTH
COL_C = 2 * CONV_WIDTH
COL_Q = 3 * CONV_WIDTH
COL_KVC = COL_Q + ATTN_WIDTH
COL_KV4 = COL_KVC + 2 * KV_WIDTH
COL_GATE = COL_KV4 + 4 * KV_WIDTH
IN_COLS_PAD = COL_GATE + N_KV_HEADS * LANES
KV8_WIDTH = 2 * 2 * N_KV_HEADS * LANES
ONES_LANE = HEAD_DIM
Q_SCALE = HEAD_DIM ** -0.5 * float(np.log2(np.e))

TM_IN = 512
TM_OUT = 512
TM_FFN = 1024
TF_FFN = 256
TM_PLE = 512
CMP_ROWS = 512
SEL_CHUNK = 512


def _dot(a, b):
    return jnp.dot(a, b, preferred_element_type=F32)


def _dot_nt(a, b):
    return lax.dot_general(a, b, (((1,), (1,)), ((), ())), preferred_element_type=F32)


def _sigmoid(x):
    return 1.0 / (1.0 + jnp.exp(-x))


def _rms(x, g):
    return x * lax.rsqrt(jnp.mean(x * x, axis=-1, keepdims=True) + EPS) * g


def _rope_slab(y, cos, sin_signed, first_half):
    fwd = pltpu.roll(y, HALF_DIM, 1)
    bwd = pltpu.roll(y, LANES - HALF_DIM, 1)
    return y * cos + jnp.where(first_half, bwd, fwd) * sin_signed


def _shift_rows(a, halo, seq_start):
    rows = a.shape[0]
    halo = jnp.where(seq_start, 0.0, halo)
    h1 = halo[BF16_ROWS - 1:BF16_ROWS, :]
    h2 = halo[BF16_ROWS - 2:BF16_ROWS - 1, :]
    row = lax.broadcasted_iota(jnp.int32, a.shape, 0)
    s1 = jnp.where(row == 0, h1, pltpu.roll(a, 1, 0))
    s2 = jnp.where(row == 0, h2, jnp.where(row == 1, h1, pltpu.roll(a, 2, 0)))
    del rows
    return s1, s2


def _inproj_kernel(x_ref, g_ref, w_ref, cos_ref, sin_ref, hot_ref,
                   cx_ref, bg_ref, q_ref, kvc_ref, kv8_ref, gate_ref):
    n = _rms(x_ref[...], g_ref[...]).astype(BF16)

    def proj(lo, hi):
        return _dot(n, w_ref[:, lo:hi])

    x_in = proj(COL_XIN, COL_B)
    bg_ref[...] = proj(COL_B, COL_C).astype(BF16)
    cx_ref[...] = (proj(COL_C, COL_Q) * x_in).astype(BF16)

    cos = cos_ref[...]
    sin = sin_ref[...]
    lane = lax.broadcasted_iota(jnp.int32, cos.shape, 1)
    first_half = (lane % HEAD_DIM) < HALF_DIM
    for s in range(ATTN_WIDTH // LANES):
        y = proj(COL_Q + s * LANES, COL_Q + (s + 1) * LANES)
        q_ref[:, s * LANES:(s + 1) * LANES] = (_rope_slab(y, cos, sin, first_half) * Q_SCALE).astype(BF16)

    kvc_ref[...] = proj(COL_KVC, COL_KV4).astype(BF16)

    in_head = lane < HEAD_DIM
    ones_col = jnp.where(lane == ONES_LANE, 1.0, 0.0)
    zeros = jnp.zeros_like(cos)
    for branch, key_fill in enumerate((hot_ref[...], zeros)):
        k_pair = _rope_slab(proj(COL_KV4 + (2 * branch) * LANES, COL_KV4 + (2 * branch + 1) * LANES),
                            cos, sin, first_half)
        v_pair = proj(COL_KV4 + (2 * branch + 1) * LANES, COL_KV4 + (2 * branch + 2) * LANES)
        for h in range(N_KV_HEADS):
            k_h = k_pair if h == 0 else pltpu.roll(k_pair, HEAD_DIM, 1)
            v_h = v_pair if h == 0 else pltpu.roll(v_pair, HEAD_DIM, 1)
            slab = (branch * N_KV_HEADS + h) * 2
            kv8_ref[:, slab * LANES:(slab + 1) * LANES] = jnp.where(in_head, k_h, key_fill).astype(BF16)
            kv8_ref[:, (slab + 1) * LANES:(slab + 2) * LANES] = jnp.where(in_head, v_h, ones_col).astype(BF16)

    gate_ref[...] = _sigmoid(proj(COL_GATE, IN_COLS_PAD))


def _inproj(x2, g_mix, w_perm, cos_q, sin_q, blk_hot, seq):
    t = x2.shape[0]
    tm = TM_IN
    n_seq_tiles = seq // tm
    row = lambda i: (i, 0)
    fixed = lambda i: (0, 0)
    tab = lambda i: (i % n_seq_tiles, 0)
    return pl.pallas_call(
        _inproj_kernel,
        grid=(t // tm,),
        in_specs=[
            pl.BlockSpec((tm, D_MODEL), row),
            pl.BlockSpec((1, D_MODEL), fixed),
            pl.BlockSpec((D_MODEL, IN_COLS_PAD), fixed),
            pl.BlockSpec((tm, LANES), tab),
            pl.BlockSpec((tm, LANES), tab),
            pl.BlockSpec((tm, LANES), tab),
        ],
        out_specs=[
            pl.BlockSpec((tm, CONV_WIDTH), row),
            pl.BlockSpec((tm, CONV_WIDTH), row),
            pl.BlockSpec((tm, ATTN_WIDTH), row),
            pl.BlockSpec((tm, 2 * KV_WIDTH), row),
            pl.BlockSpec((tm, KV8_WIDTH), row),
            pl.BlockSpec((tm, N_KV_HEADS * LANES), row),
        ],
        out_shape=[
            jax.ShapeDtypeStruct((t, CONV_WIDTH), BF16),
            jax.ShapeDtypeStruct((t, CONV_WIDTH), BF16),
            jax.ShapeDtypeStruct((t, ATTN_WIDTH), BF16),
            jax.ShapeDtypeStruct((t, 2 * KV_WIDTH), BF16),
            jax.ShapeDtypeStruct((t, KV8_WIDTH), BF16),
            jax.ShapeDtypeStruct((t, N_KV_HEADS * LANES), F32),
        ],
        compiler_params=pltpu.CompilerParams(
            dimension_semantics=("arbitrary",), vmem_limit_bytes=VMEM_LIMIT),
        name="inproj",
    )(x2, g_mix, w_perm, cos_q, sin_q, blk_hot)


def _compress_kernel(r_ref, w1_ref, pe_ref, w2_ref, cos_ref, sin_ref, out_ref):
    width = 2 * KV_WIDTH
    p = _dot(r_ref[...], w1_ref[...])
    pb = _dot(pe_ref[...], w1_ref[...])
    bias = pb[0:1, :width] + pb[1:2, width:]
    rows = p.shape[0]
    nxt = pltpu.roll(p[:, width:], rows - 1, 0)
    pre = p[:, :width] + nxt + bias
    hid = (pre * _sigmoid(pre)).astype(BF16)
    y = _dot(hid, w2_ref[...])
    cos = cos_ref[...]
    sin = sin_ref[...]
    lane = lax.broadcasted_iota(jnp.int32, (rows, LANES), 1)
    first_half = (lane % HEAD_DIM) < HALF_DIM
    for s in range(y.shape[1] // LANES):
        sl = slice(s * LANES, (s + 1) * LANES)
        out_ref[:, sl] = _rope_slab(y[:, sl], cos[:, sl], sin[:, sl], first_half).astype(BF16)


def _compress(r, w1_big, pe_big, w2_big, cos_c, sin_c):
    rows = r.shape[0]
    width = w2_big.shape[1]
    row = lambda i: (i, 0)
    fixed = lambda i: (0, 0)
    return pl.pallas_call(
        _compress_kernel,
        grid=(rows // CMP_ROWS,),
        in_specs=[
            pl.BlockSpec((CMP_ROWS, r.shape[1]), row),
            pl.BlockSpec(w1_big.shape, fixed),
            pl.BlockSpec(pe_big.shape, fixed),
            pl.BlockSpec(w2_big.shape, fixed),
            pl.BlockSpec((CMP_ROWS, width), fixed),
            pl.BlockSpec((CMP_ROWS, width), fixed),
        ],
        out_specs=pl.BlockSpec((CMP_ROWS, width), row),
        out_shape=jax.ShapeDtypeStruct((rows, width), BF16),
        compiler_params=pltpu.CompilerParams(
            dimension_semantics=("arbitrary",), vmem_limit_bytes=VMEM_LIMIT),
        name="compress",
    )(r, w1_big, pe_big, w2_big, cos_c, sin_c)


def _masked_softmax2(s, mask):
    s = jnp.where(mask, s, NEG)
    m = jnp.max(s, axis=-1, keepdims=True)
    e = jnp.where(mask, jnp.exp2(s - m), 0.0)
    return e / jnp.maximum(jnp.sum(e, axis=-1, keepdims=True), 1e-30)


def _tile_groups(a):
    return jnp.concatenate([a] * GROUP_SIZE, axis=0)


def _online_softmax2_step(carry, s, v):
    m_i, acc = carry
    m_new = jnp.maximum(m_i, jnp.max(s, axis=-1, keepdims=True))
    e = jnp.exp2(s - m_new).astype(BF16)
    return m_new, jnp.exp2(m_i - m_new) * acc + _dot(e, v)


def _nsa_kernel(q_ref, gate_ref, kc_ref, vc_ref, ks_ref, vs_ref, kw_ref, vw_ref, ovl_ref, tri_ref, band_ref,
                o_ref):
    i = pl.program_id(2)
    q0 = i * Q_BLOCK
    g_rows = GROUP_SIZE * Q_BLOCK

    qf = q_ref[...].astype(F32)
    lane_q = lax.broadcasted_iota(jnp.int32, (Q_BLOCK, LANES), 1)
    parts = []
    for g in range(GROUP_SIZE):
        slab = qf[:, (g // 2) * LANES:(g // 2 + 1) * LANES]
        if g % 2 == 1:
            slab = pltpu.roll(slab, HEAD_DIM, 1)
        parts.append(jnp.where(lane_q < HEAD_DIM, slab, 0.0))
    qs = jnp.concatenate(parts, axis=0).astype(BF16)

    kc = kc_ref[...]
    n_cmp = kc.shape[0]
    s_c = _dot_nt(qs, kc)
    t_c = q0 + lax.broadcasted_iota(jnp.int32, (g_rows, n_cmp), 0) % Q_BLOCK
    end_c = lax.broadcasted_iota(jnp.int32, (g_rows, n_cmp), 1) * CMP_STRIDE + (CMP_LEN - 1)
    p_c = _masked_softmax2(s_c, end_c <= t_c)
    o_c = _dot(p_c.astype(BF16), vc_ref[...])

    psum = p_c[0:Q_BLOCK]
    for g in range(1, GROUP_SIZE):
        psum = psum + p_c[g * Q_BLOCK:(g + 1) * Q_BLOCK]
    p_hi = psum.astype(BF16)
    p_lo = (psum - p_hi.astype(F32)).astype(BF16)
    ovl = ovl_ref[...]
    imp = _dot_nt(ovl, p_hi) + _dot_nt(ovl, p_lo)
    n_blk = imp.shape[0]
    blk = lax.broadcasted_iota(jnp.int32, (n_blk, Q_BLOCK), 0)
    t_q = q0 + lax.broadcasted_iota(jnp.int32, (n_blk, Q_BLOCK), 1)
    cur = t_q // SEL_BLOCK
    valid = blk * SEL_BLOCK <= t_q
    forced = (blk == 0) | (blk == cur) | (blk == cur - 1)
    score = jnp.where(valid, imp + jnp.where(forced, SEL_FORCE, 0.0), NEG)
    rank = jnp.zeros((n_blk, Q_BLOCK), F32)
    for m in range(n_blk):
        other = score[m:m + 1, :]
        tie = jnp.where(blk > m, 1.0, 0.0)
        rank = rank + jnp.where(other > score, 1.0, 0.0) + jnp.where(other == score, tie, 0.0)
    sel_neg = jnp.where((rank < float(min(N_SEL, n_blk))) & valid, 0.0, NEG)
    pad_lo = jnp.zeros((HEAD_DIM, Q_BLOCK), F32)
    pad_hi = jnp.zeros((LANES - HEAD_DIM - n_blk, Q_BLOCK), F32)
    sel_rows = jnp.concatenate([pad_lo, sel_neg, pad_hi], axis=0).T
    qa = jnp.concatenate([p + sel_rows for p in parts], axis=0).astype(BF16)

    def sel_scores(c):
        k0 = pl.multiple_of(c * SEL_CHUNK, SEL_CHUNK)
        return _dot_nt(qa, ks_ref[pl.ds(k0, SEL_CHUNK), :]), vs_ref[pl.ds(k0, SEL_CHUNK), :]

    def sel_step(c, carry):
        return _online_softmax2_step(carry, *sel_scores(c))

    last = q0 // SEL_CHUNK
    carry = (jnp.full((g_rows, 1), NEG, F32), jnp.zeros((g_rows, LANES), F32))
    carry = lax.fori_loop(0, last, sel_step, carry)
    s_last, v_last = sel_scores(last)
    s_last = s_last + _tile_groups(tri_ref[i % (SEL_CHUNK // Q_BLOCK)])
    _, acc_s = _online_softmax2_step(carry, s_last, v_last)
    o_s = acc_s / jnp.maximum(acc_s[:, ONES_LANE:ONES_LANE + 1], 1e-30)

    n_win = Q_BLOCK + WINDOW
    n_band = band_ref.shape[0]
    w0 = pl.multiple_of(jnp.maximum(q0 - WINDOW, 0), Q_BLOCK)
    s_w = _dot_nt(qs, kw_ref[pl.ds(w0, n_win), :]) + _tile_groups(band_ref[jnp.minimum(i, n_band - 1)])
    e_w = jnp.exp2(s_w - jnp.max(s_w, axis=-1, keepdims=True)).astype(BF16)
    acc_w = _dot(e_w, vw_ref[pl.ds(w0, n_win), :])
    o_w = acc_w / jnp.maximum(acc_w[:, ONES_LANE:ONES_LANE + 1], 1e-30)

    gate = gate_ref[...]

    def gate_col(r):
        return jnp.concatenate(
            [gate[:, g * N_BRANCH + r:g * N_BRANCH + r + 1] for g in range(GROUP_SIZE)], axis=0)

    o = gate_col(0) * o_c + gate_col(1) * o_s + gate_col(2) * o_w
    for j in range(GROUP_SIZE // 2):
        even = o[(2 * j) * Q_BLOCK:(2 * j + 1) * Q_BLOCK]
        odd = pltpu.roll(o[(2 * j + 1) * Q_BLOCK:(2 * j + 2) * Q_BLOCK], HEAD_DIM, 1)
        o_ref[:, j * LANES:(j + 1) * LANES] = jnp.where(lane_q < HEAD_DIM, even, odd).astype(BF16)


def _nsa(q, gates, kcv, kv8, ovl_t, tri, band, batch, seq):
    n_qb = seq // Q_BLOCK
    n_cmp_pad = kcv.shape[0] // batch
    grp = GROUP_SIZE * HEAD_DIM
    q_map = lambda b, h, i: (b * n_qb + i, h)
    fixed2 = lambda b, h, i: (0, 0)
    fixed3 = lambda b, h, i: (0, 0, 0)

    def kv_spec(rows, slab):
        return pl.BlockSpec((rows, LANES), lambda b, h, i: (b, 2 * h + slab))

    return pl.pallas_call(
        _nsa_kernel,
        grid=(batch, N_KV_HEADS, n_qb),
        in_specs=[
            pl.BlockSpec((Q_BLOCK, grp), q_map),
            pl.BlockSpec((Q_BLOCK, LANES), q_map),
            kv_spec(n_cmp_pad, 0),
            kv_spec(n_cmp_pad, 1),
            kv_spec(seq, 0),
            kv_spec(seq, 1),
            kv_spec(seq, 2 * N_KV_HEADS),
            kv_spec(seq, 2 * N_KV_HEADS + 1),
            pl.BlockSpec(ovl_t.shape, fixed2),
            pl.BlockSpec(tri.shape, fixed3),
            pl.BlockSpec(band.shape, fixed3),
        ],
        out_specs=pl.BlockSpec((Q_BLOCK, grp), q_map),
        out_shape=jax.ShapeDtypeStruct((batch * seq, ATTN_WIDTH), BF16),
        compiler_params=pltpu.CompilerParams(
            dimension_semantics=("arbitrary", "arbitrary", "arbitrary"), vmem_limit_bytes=VMEM_LIMIT),
        name="nsa",
    )(q, gates, kcv, kcv, kv8, kv8, kv8, kv8, ovl_t, tri, band)


def _outproj_kernel(cx_ref, cxh_ref, bg_ref, y_ref, x_ref, wc_ref, gc_ref, ga_ref, w_ref, gf_ref,
                    h_ref, n2_ref, *, tiles_per_seq):
    seq_start = pl.program_id(0) % tiles_per_seq == 0
    cx = cx_ref[...].astype(F32)
    s1, s2 = _shift_rows(cx, cxh_ref[...].astype(F32), seq_start)
    wc = wc_ref[...]
    conv = wc[0:1] * s2 + wc[1:2] * s1 + wc[2:3] * cx
    y_conv = bg_ref[...].astype(F32) * conv
    mixed = jnp.concatenate(
        [_rms(y_conv, gc_ref[...]).astype(BF16), _rms(y_ref[...].astype(F32), ga_ref[...]).astype(BF16)], axis=1)
    h = x_ref[...] + _dot(mixed, w_ref[...])
    h_ref[...] = h
    n2_ref[...] = _rms(h, gf_ref[...]).astype(BF16)


def _outproj(cx, bg, y_attn, x2, w_conv, g_conv, g_attn, w_out, g_ffn, seq):
    t = x2.shape[0]
    tm = TM_OUT
    row = lambda i: (i, 0)
    fixed = lambda i: (0, 0)
    halo = lambda i: (jnp.maximum(i * (tm // BF16_ROWS) - 1, 0), 0)
    return pl.pallas_call(
        functools.partial(_outproj_kernel, tiles_per_seq=seq // tm),
        grid=(t // tm,),
        in_specs=[
            pl.BlockSpec((tm, CONV_WIDTH), row),
            pl.BlockSpec((BF16_ROWS, CONV_WIDTH), halo),
            pl.BlockSpec((tm, CONV_WIDTH), row),
            pl.BlockSpec((tm, ATTN_WIDTH), row),
            pl.BlockSpec((tm, D_MODEL), row),
            pl.BlockSpec(w_conv.shape, fixed),
            pl.BlockSpec((1, CONV_WIDTH), fixed),
            pl.BlockSpec((1, ATTN_WIDTH), fixed),
            pl.BlockSpec(w_out.shape, fixed),
            pl.BlockSpec((1, D_MODEL), fixed),
        ],
        out_specs=[pl.BlockSpec((tm, D_MODEL), row), pl.BlockSpec((tm, D_MODEL), row)],
        out_shape=[jax.ShapeDtypeStruct((t, D_MODEL), F32), jax.ShapeDtypeStruct((t, D_MODEL), BF16)],
        compiler_params=pltpu.CompilerParams(
            dimension_semantics=("arbitrary",), vmem_limit_bytes=VMEM_LIMIT),
        name="outproj",
    )(cx, cx, bg, y_attn, x2, w_conv, g_conv, g_attn, w_out, g_ffn)


def _ffn_kernel(n_ref, nh_ref, h_ref, wg_ref, wv_ref, cg_ref, cv_ref, wd_ref, out_ref, *, tiles_per_seq):
    seq_start = pl.program_id(0) % tiles_per_seq == 0
    j = pl.program_id(1)

    @pl.when(j == 0)
    def _():
        out_ref[...] = h_ref[...]

    n = n_ref[...]
    nh = nh_ref[...]

    def conv_up(w_ref, c_ref):
        w = w_ref[...]
        a = _dot(n, w)
        s1, s2 = _shift_rows(a, _dot(nh, w), seq_start)
        c = c_ref[...]
        return c[0:1] * s2 + c[1:2] * s1 + c[2:3] * a

    u_gate = conv_up(wg_ref, cg_ref)
    u_val = conv_up(wv_ref, cv_ref)
    act = (u_gate * _sigmoid(u_gate) * u_val).astype(BF16)
    out_ref[...] += _dot(act, wd_ref[...])


def _ffn(n2, h1, w_up, w_conv, w_down, seq):
    t = n2.shape[0]
    tm, tf = TM_FFN, TF_FFN
    n_f = D_FF // tf
    row = lambda i, j: (i, 0)
    halo = lambda i, j: (jnp.maximum(i * (tm // BF16_ROWS) - 1, 0), 0)
    return pl.pallas_call(
        functools.partial(_ffn_kernel, tiles_per_seq=seq // tm),
        grid=(t // tm, n_f),
        in_specs=[
            pl.BlockSpec((tm, D_MODEL), row),
            pl.BlockSpec((BF16_ROWS, D_MODEL), halo),
            pl.BlockSpec((tm, D_MODEL), row),
            pl.BlockSpec((D_MODEL, tf), lambda i, j: (0, j)),
            pl.BlockSpec((D_MODEL, tf), lambda i, j: (0, n_f + j)),
            pl.BlockSpec((w_conv.shape[0], tf), lambda i, j: (0, j)),
            pl.BlockSpec((w_conv.shape[0], tf), lambda i, j: (0, n_f + j)),
            pl.BlockSpec((tf, D_MODEL), lambda i, j: (j, 0)),
        ],
        out_specs=pl.BlockSpec((tm, D_MODEL), row),
        out_shape=jax.ShapeDtypeStruct((t, D_MODEL), F32),
        compiler_params=pltpu.CompilerParams(
            dimension_semantics=("arbitrary", "arbitrary"), vmem_limit_bytes=VMEM_LIMIT),
        name="ffn",
    )(n2, n2, h1, w_up, w_up, w_conv, w_conv, w_down)


def _ple_kernel(h_ref, p_ref, gp_ref, wg_ref, wp_ref, gfin_ref, out_ref):
    h = h_ref[...]
    gate = _sigmoid(_dot(_rms(h, gp_ref[...]).astype(BF16), wg_ref[...]))
    h = h + gate * _dot(p_ref[...].astype(BF16), wp_ref[...])
    out_ref[...] = _rms(h, gfin_ref[...])


def _ple(h2, p2, g_ple, w_gate, w_proj, g_final):
    t = h2.shape[0]
    tm = TM_PLE
    row = lambda i: (i, 0)
    fixed = lambda i: (0, 0)
    return pl.pallas_call(
        _ple_kernel,
        grid=(t // tm,),
        in_specs=[
            pl.BlockSpec((tm, D_MODEL), row),
            pl.BlockSpec((tm, PLE_DIM), row),
            pl.BlockSpec((1, D_MODEL), fixed),
            pl.BlockSpec(w_gate.shape, fixed),
            pl.BlockSpec(w_proj.shape, fixed),
            pl.BlockSpec((1, D_MODEL), fixed),
        ],
        out_specs=pl.BlockSpec((tm, D_MODEL), row),
        out_shape=jax.ShapeDtypeStruct((t, D_MODEL), F32),
        compiler_params=pltpu.CompilerParams(
            dimension_semantics=("arbitrary",), vmem_limit_bytes=VMEM_LIMIT),
        name="ple",
    )(h2, p2, g_ple, w_gate, w_proj, g_final)


def _in_col_perm():
    in_cols = 3 * CONV_WIDTH + ATTN_WIDTH + 2 * N_BRANCH * KV_WIDTH + N_BRANCH * N_HEADS
    base_gate = in_cols - N_BRANCH * N_HEADS
    assert base_gate == COL_GATE
    idx = list(range(COL_GATE))
    per_kv = GROUP_SIZE * N_BRANCH
    for h in range(N_KV_HEADS):
        idx += list(range(base_gate + h * per_kv, base_gate + (h + 1) * per_kv))
        idx += [in_cols] * (LANES - per_kv)
    assert len(idx) == IN_COLS_PAD
    return np.asarray(idx, np.int32), in_cols


def _rope_tables(pos):
    inv = ROPE_THETA ** (-jnp.arange(0, HEAD_DIM, 2, dtype=F32) / HEAD_DIM)
    ang = pos.astype(F32)[:, None] * inv[None, :]
    cos = jnp.concatenate([jnp.cos(ang), jnp.cos(ang)], axis=-1)
    sin = jnp.concatenate([-jnp.sin(ang), jnp.sin(ang)], axis=-1)
    return cos, sin


def _compress_weights(w1_k, w1_v, w2_k, w2_v, pe_k, pe_v):
    n_str = 2 * N_KV_HEADS
    half = CMP_LEN // 2
    w1 = jnp.stack([w1_k, w1_k, w1_v, w1_v]).reshape(n_str, 2, half, HEAD_DIM, HEAD_DIM)
    eye = jnp.eye(n_str, dtype=F32)
    w1_big = jnp.einsum('chrde,cx->rcdhxe', w1, eye).reshape(half * n_str * HEAD_DIM, 2 * n_str * HEAD_DIM)
    pe = jnp.stack([pe_k, pe_k, pe_v, pe_v]).reshape(n_str, 2, half, HEAD_DIM)
    pe_big = jnp.transpose(pe, (1, 2, 0, 3)).reshape(2, half * n_str * HEAD_DIM)
    pe_big = jnp.concatenate([pe_big, jnp.zeros((6, pe_big.shape[1]), F32)], axis=0)
    w2 = jnp.stack([w2_k, w2_k, w2_v, w2_v])
    out_pos = np.asarray([0, 4, 2, 6])
    place = jnp.zeros((n_str, 2 * n_str), F32).at[np.arange(n_str), out_pos].set(1.0)
    w2_big = jnp.einsum('cde,cx->cdxe', w2, place).reshape(n_str * HEAD_DIM, 2 * n_str * HEAD_DIM)
    return w1_big.astype(BF16), pe_big.astype(BF16), w2_big.astype(BF16)


def kernel(x, p, g_mix, w_in, w_conv_mix, cmp_pe_k, cmp_w1_k, cmp_w2_k, cmp_pe_v, cmp_w1_v, cmp_w2_v,
           g_gn_conv, g_gn_attn, w_out, g_ffn, w_up, w_ffn_conv, w_down, g_ple, w_ple_gate, w_ple_proj,
           g_final):
    batch, seq, _ = x.shape
    depth = w_in.shape[0]
    t = batch * seq
    n_cmp_pad = seq // CMP_STRIDE
    n_blk = seq // SEL_BLOCK

    cos, sin = _rope_tables(jnp.arange(seq, dtype=jnp.int32))
    cos_q = jnp.concatenate([cos, cos], axis=-1)
    sin_q = jnp.concatenate([sin, sin], axis=-1)
    cmp_start = jnp.arange(n_cmp_pad) * CMP_STRIDE
    cos_e, sin_e = _rope_tables(cmp_start + CMP_LEN - 1)
    ones = jnp.ones_like(cos_e)
    zeros = jnp.zeros_like(sin_e)
    reps = CMP_ROWS // n_cmp_pad
    cos_c = jnp.tile(jnp.concatenate([cos_e, ones, ones, ones] * N_KV_HEADS, axis=-1), (reps, 1))
    sin_c = jnp.tile(jnp.concatenate([sin_e, zeros, zeros, zeros] * N_KV_HEADS, axis=-1), (reps, 1))
    blk_start = jnp.arange(n_blk) * SEL_BLOCK
    n_cmp = (seq - CMP_LEN) // CMP_STRIDE + 1
    ovl = (jnp.clip(jnp.minimum(cmp_start[:, None] + CMP_LEN, blk_start[None, :] + SEL_BLOCK)
                    - jnp.maximum(cmp_start[:, None], blk_start[None, :]), 0, None).astype(F32) / CMP_LEN)
    ovl = jnp.where(jnp.arange(n_cmp_pad)[:, None] < n_cmp, ovl, 0.0)
    ovl_t = ovl.T.astype(BF16)
    lane = jnp.arange(LANES)[None, :]
    blk_hot = (lane - HEAD_DIM == jnp.arange(seq)[:, None] // SEL_BLOCK).astype(F32)
    qi = jnp.arange(Q_BLOCK)[None, :, None]
    n_tri = SEL_CHUNK // Q_BLOCK
    tri = jnp.where(jnp.arange(SEL_CHUNK)[None, None, :] <= jnp.arange(n_tri)[:, None, None] * Q_BLOCK + qi,
                    0.0, NEG).astype(F32)
    n_band = WINDOW // Q_BLOCK
    kk = jnp.arange(Q_BLOCK + WINDOW)[None, None, :]
    t_rel = jnp.minimum(jnp.arange(n_band + 1), n_band)[:, None, None] * Q_BLOCK + qi
    band = jnp.where((kk <= t_rel) & (kk > t_rel - WINDOW), 0.0, NEG).astype(F32)

    perm, in_cols = _in_col_perm()
    pad_rows = lambda w: jnp.concatenate([w, jnp.zeros((8 - w.shape[0], w.shape[1]), w.dtype)], axis=0)

    h = x.reshape(t, D_MODEL)
    for i in range(depth):
        w_ext = jnp.concatenate([w_in[i], jnp.zeros((D_MODEL, 1), F32)], axis=1)
        w_perm = jnp.take(w_ext, perm, axis=1).astype(BF16)
        cx, bg, q, kvc, kv8, gates = _inproj(h, g_mix[i][None, :], w_perm, cos_q, sin_q, blk_hot, seq)

        w1_big, pe_big, w2_big = _compress_weights(
            cmp_w1_k[i], cmp_w1_v[i], cmp_w2_k[i], cmp_w2_v[i], cmp_pe_k[i], cmp_pe_v[i])
        r = kvc.reshape(t // CMP_STRIDE, CMP_STRIDE * 2 * KV_WIDTH)
        kcv = _compress(r, w1_big, pe_big, w2_big, cos_c, sin_c)

        y_attn = _nsa(q, gates, kcv, kv8, ovl_t, tri, band, batch, seq)

        h1, n2 = _outproj(cx, bg, y_attn, h, pad_rows(w_conv_mix[i]), g_gn_conv[i][None, :],
                          g_gn_attn[i][None, :], w_out[i].astype(BF16), g_ffn[i][None, :], seq)

        h2 = _ffn(n2, h1, w_up[i].astype(BF16), pad_rows(w_ffn_conv[i]), w_down[i].astype(BF16), seq)

        assert depth == 1
        h = _ple(h2, p[i].reshape(t, PLE_DIM), g_ple[i][None, :], w_ple_gate[i].astype(BF16),
                 w_ple_proj[i].astype(BF16), g_final[None, :])
    return h.reshape(batch, seq, D_MODEL)
```

```python
import functools

import numpy as np
import jax
import jax.numpy as jnp
from jax import lax
from jax.experimental import pallas as pl
from jax.experimental.pallas import tpu as pltpu

F32 = jnp.float32
BF16 = jnp.bfloat16

D_MODEL = 1024
PLE_DIM = 256
CONV_WIDTH = 512
N_HEADS = 8
N_KV_HEADS = 2
HEAD_DIM = 64
HALF_DIM = HEAD_DIM // 2
GROUP_SIZE = N_HEADS // N_KV_HEADS
ATTN_WIDTH = N_HEADS * HEAD_DIM
KV_WIDTH = N_KV_HEADS * HEAD_DIM
N_BRANCH = 3
CMP_LEN = 32
CMP_STRIDE = 16
SEL_BLOCK = 64
N_SEL = 8
WINDOW = 512
Q_BLOCK = 128
D_FF = 2816
ROPE_THETA = 10000.0
EPS = 1e-6
NEG = -1e30
SEL_FORCE = 1e4

LANES = 128
BF16_ROWS = 16
VMEM_LIMIT = 56 * 1024 * 1024

COL_XIN = 0
COL_B = CONV_WIDTH
COL_C = 2 * CONV_WIDTH
COL_Q = 3 * CONV_WIDTH
COL_KVC = COL_Q + ATTN_WIDTH
COL_KV4 = COL_KVC + 2 * KV_WIDTH
COL_GATE = COL_KV4 + 4 * KV_WIDTH
IN_COLS_PAD = COL_GATE + N_KV_HEADS * LANES
KV8_WIDTH = 2 * 2 * N_KV_HEADS * LANES
ONES_LANE = HEAD_DIM
Q_SCALE = HEAD_DIM ** -0.5 * float(np.log2(np.e))

TM_IN = 512
TM_FFN = 512
TF_FFN = 256
CMP_ROWS = 512
SEL_CHUNK = 512
SEL_TILE = 512


def _dot(a, b):
    return jnp.dot(a, b, preferred_element_type=F32)


def _dot_nt(a, b):
    return lax.dot_general(a, b, (((1,), (1,)), ((), ())), preferred_element_type=F32)


def _sigmoid(x):
    return 1.0 / (1.0 + jnp.exp(-x))


def _rms(x, g):
    return x * lax.rsqrt(jnp.mean(x * x, axis=-1, keepdims=True) + EPS) * g


def _rope_slab(y, cos, sin_signed, first_half):
    fwd = pltpu.roll(y, HALF_DIM, 1)
    bwd = pltpu.roll(y, LANES - HALF_DIM, 1)
    return y * cos + jnp.where(first_half, bwd, fwd) * sin_signed


def _shift_rows(a, halo, seq_start):
    halo = jnp.where(seq_start, 0.0, halo)
    h1 = halo[BF16_ROWS - 1:BF16_ROWS, :]
    h2 = halo[BF16_ROWS - 2:BF16_ROWS - 1, :]
    row = lax.broadcasted_iota(jnp.int32, a.shape, 0)
    s1 = jnp.where(row == 0, h1, pltpu.roll(a, 1, 0))
    s2 = jnp.where(row == 0, h2, jnp.where(row == 1, h1, pltpu.roll(a, 2, 0)))
    return s1, s2


def _inproj_kernel(x_ref, g_ref, w_ref, cos_ref, sin_ref, hot_ref,
                   cx_ref, bg_ref, q_ref, kvc_ref, kv8_ref, gate_ref):
    n = _rms(x_ref[...], g_ref[...]).astype(BF16)

    def proj(lo, hi):
        return _dot(n, w_ref[:, lo:hi])

    x_in = proj(COL_XIN, COL_B)
    bg_ref[...] = proj(COL_B, COL_C).astype(BF16)
    cx_ref[...] = (proj(COL_C, COL_Q) * x_in).astype(BF16)

    cos = cos_ref[...]
    sin = sin_ref[...]
    lane = lax.broadcasted_iota(jnp.int32, cos.shape, 1)
    first_half = (lane % HEAD_DIM) < HALF_DIM
    for s in range(ATTN_WIDTH // LANES):
        y = proj(COL_Q + s * LANES, COL_Q + (s + 1) * LANES)
        q_ref[:, s * LANES:(s + 1) * LANES] = (_rope_slab(y, cos, sin, first_half) * Q_SCALE).astype(BF16)

    kvc_ref[...] = proj(COL_KVC, COL_KV4).astype(BF16)

    in_head = lane < HEAD_DIM
    ones_col = jnp.where(lane == ONES_LANE, 1.0, 0.0)
    zeros = jnp.zeros_like(cos)
    for branch, key_fill in enumerate((hot_ref[...], zeros)):
        k_pair = _rope_slab(proj(COL_KV4 + (2 * branch) * LANES, COL_KV4 + (2 * branch + 1) * LANES),
                            cos, sin, first_half)
        v_pair = proj(COL_KV4 + (2 * branch + 1) * LANES, COL_KV4 + (2 * branch + 2) * LANES)
        for h in range(N_KV_HEADS):
            k_h = k_pair if h == 0 else pltpu.roll(k_pair, HEAD_DIM, 1)
            v_h = v_pair if h == 0 else pltpu.roll(v_pair, HEAD_DIM, 1)
            slab = (branch * N_KV_HEADS + h) * 2
            kv8_ref[:, slab * LANES:(slab + 1) * LANES] = jnp.where(in_head, k_h, key_fill).astype(BF16)
            kv8_ref[:, (slab + 1) * LANES:(slab + 2) * LANES] = jnp.where(in_head, v_h, ones_col).astype(BF16)

    gate_ref[...] = _sigmoid(proj(COL_GATE, IN_COLS_PAD))


def _inproj(x2, g_mix, w_perm, cos_q, sin_q, blk_hot, seq):
    t = x2.shape[0]
    tm = TM_IN
    n_seq_tiles = seq // tm
    row = lambda i: (i, 0)
    fixed = lambda i: (0, 0)
    tab = lambda i: (i % n_seq_tiles, 0)
    return pl.pallas_call(
        _inproj_kernel,
        grid=(t // tm,),
        in_specs=[
            pl.BlockSpec((tm, D_MODEL), row),
            pl.BlockSpec((1, D_MODEL), fixed),
            pl.BlockSpec((D_MODEL, IN_COLS_PAD), fixed),
            pl.BlockSpec((tm, LANES), tab),
            pl.BlockSpec((tm, LANES), tab),
            pl.BlockSpec((tm, LANES), tab),
        ],
        out_specs=[
            pl.BlockSpec((tm, CONV_WIDTH), row),
            pl.BlockSpec((tm, CONV_WIDTH), row),
            pl.BlockSpec((tm, ATTN_WIDTH), row),
            pl.BlockSpec((tm, 2 * KV_WIDTH), row),
            pl.BlockSpec((tm, KV8_WIDTH), row),
            pl.BlockSpec((tm, N_KV_HEADS * LANES), row),
        ],
        out_shape=[
            jax.ShapeDtypeStruct((t, CONV_WIDTH), BF16),
            jax.ShapeDtypeStruct((t, CONV_WIDTH), BF16),
            jax.ShapeDtypeStruct((t, ATTN_WIDTH), BF16),
            jax.ShapeDtypeStruct((t, 2 * KV_WIDTH), BF16),
            jax.ShapeDtypeStruct((t, KV8_WIDTH), BF16),
            jax.ShapeDtypeStruct((t, N_KV_HEADS * LANES), F32),
        ],
        compiler_params=pltpu.CompilerParams(
            dimension_semantics=("arbitrary",), vmem_limit_bytes=VMEM_LIMIT),
        name="inproj",
    )(x2, g_mix, w_perm, cos_q, sin_q, blk_hot)


def _compress_kernel(r_ref, w1_ref, pe_ref, w2_ref, cos_ref, sin_ref, out_ref):
    width = 2 * KV_WIDTH
    p = _dot(r_ref[...], w1_ref[...])
    pb = _dot(pe_ref[...], w1_ref[...])
    bias = pb[0:1, :width] + pb[1:2, width:]
    rows = p.shape[0]
    nxt = pltpu.roll(p[:, width:], rows - 1, 0)
    pre = p[:, :width] + nxt + bias
    hid = (pre * _sigmoid(pre)).astype(BF16)
    y = _dot(hid, w2_ref[...])
    cos = cos_ref[...]
    sin = sin_ref[...]
    lane = lax.broadcasted_iota(jnp.int32, (rows, LANES), 1)
    first_half = (lane % HEAD_DIM) < HALF_DIM
    for s in range(y.shape[1] // LANES):
        sl = slice(s * LANES, (s + 1) * LANES)
        out_ref[:, sl] = _rope_slab(y[:, sl], cos[:, sl], sin[:, sl], first_half).astype(BF16)


def _compress(r, w1_big, pe_big, w2_big, cos_c, sin_c):
    rows = r.shape[0]
    width = w2_big.shape[1]
    row = lambda i: (i, 0)
    fixed = lambda i: (0, 0)
    return pl.pallas_call(
        _compress_kernel,
        grid=(rows // CMP_ROWS,),
        in_specs=[
            pl.BlockSpec((CMP_ROWS, r.shape[1]), row),
            pl.BlockSpec(w1_big.shape, fixed),
            pl.BlockSpec(pe_big.shape, fixed),
            pl.BlockSpec(w2_big.shape, fixed),
            pl.BlockSpec((CMP_ROWS, width), fixed),
            pl.BlockSpec((CMP_ROWS, width), fixed),
        ],
        out_specs=pl.BlockSpec((CMP_ROWS, width), row),
        out_shape=jax.ShapeDtypeStruct((rows, width), BF16),
        compiler_params=pltpu.CompilerParams(
            dimension_semantics=("arbitrary",), vmem_limit_bytes=VMEM_LIMIT),
        name="compress",
    )(r, w1_big, pe_big, w2_big, cos_c, sin_c)


def _masked_softmax2(s, mask):
    s = jnp.where(mask, s, NEG)
    m = jnp.max(s, axis=-1, keepdims=True)
    e = jnp.where(mask, jnp.exp2(s - m), 0.0)
    return e / jnp.maximum(jnp.sum(e, axis=-1, keepdims=True), 1e-30)


def _softmax2_av(scores, values):
    m = functools.reduce(jnp.maximum, [jnp.max(s, axis=-1, keepdims=True) for s in scores])
    acc = functools.reduce(jnp.add, [_dot(jnp.exp2(s - m).astype(BF16), v) for s, v in zip(scores, values)])
    return acc / jnp.maximum(acc[:, ONES_LANE:ONES_LANE + 1], 1e-30)


def _stack_heads(q, lane):
    parts = []
    for g in range(GROUP_SIZE):
        slab = q[:, (g // 2) * LANES:(g // 2 + 1) * LANES]
        if g % 2 == 1:
            slab = pltpu.roll(slab, HEAD_DIM, 1)
        parts.append(jnp.where(lane < HEAD_DIM, slab, 0.0))
    return parts


def _merge_head_pair(even, odd, lane):
    return jnp.where(lane < HEAD_DIM, even, pltpu.roll(odd, HEAD_DIM, 1))


def _nsa_select_kernel(q_ref, gate_ref, kc_ref, vc_ref, ovl_ref, qa_ref, oc_ref):
    tile = q_ref.shape[0]
    q0 = pl.program_id(2) * tile
    lane = lax.broadcasted_iota(jnp.int32, (tile, LANES), 1)
    parts = _stack_heads(q_ref[...].astype(F32), lane)
    qs = jnp.concatenate(parts, axis=0).astype(BF16)

    s_c = _dot_nt(qs, kc_ref[...])
    t_c = q0 + lax.broadcasted_iota(jnp.int32, s_c.shape, 0) % tile
    end_c = lax.broadcasted_iota(jnp.int32, s_c.shape, 1) * CMP_STRIDE + (CMP_LEN - 1)
    p_c = _masked_softmax2(s_c, end_c <= t_c)
    o_c = _dot(p_c.astype(BF16), vc_ref[...])

    psum = p_c[0:tile]
    for g in range(1, GROUP_SIZE):
        psum = psum + p_c[g * tile:(g + 1) * tile]
    p_hi = psum.astype(BF16)
    p_lo = (psum - p_hi.astype(F32)).astype(BF16)
    ovl = ovl_ref[...]
    imp = _dot_nt(ovl, p_hi) + _dot_nt(ovl, p_lo)
    n_blk = imp.shape[0]
    blk = lax.broadcasted_iota(jnp.int32, imp.shape, 0)
    t_q = q0 + lax.broadcasted_iota(jnp.int32, imp.shape, 1)
    cur = t_q // SEL_BLOCK
    valid = blk * SEL_BLOCK <= t_q
    forced = (blk == 0) | (blk == cur) | (blk == cur - 1)
    score = jnp.where(valid, imp + jnp.where(forced, SEL_FORCE, 0.0), NEG)
    rank = jnp.zeros(imp.shape, F32)
    for m in range(n_blk):
        other = score[m:m + 1, :]
        tie = jnp.where(blk > m, 1.0, 0.0)
        rank = rank + jnp.where(other > score, 1.0, 0.0) + jnp.where(other == score, tie, 0.0)
    sel_neg = jnp.where((rank < float(min(N_SEL, n_blk))) & valid, 0.0, NEG)
    pad_lo = jnp.zeros((HEAD_DIM, tile), F32)
    pad_hi = jnp.zeros((LANES - HEAD_DIM - n_blk, tile), F32)
    sel_rows = jnp.concatenate([pad_lo, sel_neg, pad_hi], axis=0).T
    for g in range(GROUP_SIZE):
        qa_ref[:, g * LANES:(g + 1) * LANES] = (parts[g] + sel_rows).astype(BF16)

    gate = gate_ref[...]
    for j in range(GROUP_SIZE // 2):
        pair = [gate[:, g * N_BRANCH:g * N_BRANCH + 1] * o_c[g * tile:(g + 1) * tile] for g in (2 * j, 2 * j + 1)]
        oc_ref[:, j * LANES:(j + 1) * LANES] = _merge_head_pair(pair[0], pair[1], lane).astype(BF16)


def _nsa_select(q, gates, kcv, ovl_t, batch, seq):
    tile = SEL_TILE
    n_t = seq // tile
    n_cmp_pad = kcv.shape[0] // batch
    grp = GROUP_SIZE * HEAD_DIM
    q_map = lambda b, h, i: (b * n_t + i, h)
    return pl.pallas_call(
        _nsa_select_kernel,
        grid=(batch, N_KV_HEADS, n_t),
        in_specs=[
            pl.BlockSpec((tile, grp), q_map),
            pl.BlockSpec((tile, LANES), q_map),
            pl.BlockSpec((n_cmp_pad, LANES), lambda b, h, i: (b, 2 * h)),
            pl.BlockSpec((n_cmp_pad, LANES), lambda b, h, i: (b, 2 * h + 1)),
            pl.BlockSpec(ovl_t.shape, lambda b, h, i: (0, 0)),
        ],
        out_specs=[pl.BlockSpec((tile, GROUP_SIZE * LANES), q_map), pl.BlockSpec((tile, grp), q_map)],
        out_shape=[jax.ShapeDtypeStruct((batch * seq, N_HEADS * LANES), BF16),
                   jax.ShapeDtypeStruct((batch * seq, ATTN_WIDTH), BF16)],
        compiler_params=pltpu.CompilerParams(
            dimension_semantics=("arbitrary", "arbitrary", "arbitrary"), vmem_limit_bytes=VMEM_LIMIT),
        name="nsa_select",
    )(q, gates, kcv, kcv, ovl_t)


def _nsa_attend_kernel(qa_ref, gate_ref, oc_ref, *refs):
    i = pl.program_id(1)
    seq = refs[0].shape[0]
    for n_full in range(seq // SEL_CHUNK):
        pl.when(i // (SEL_CHUNK // Q_BLOCK) == n_full)(functools.partial(
            _nsa_attend_body, n_full, qa_ref, gate_ref, oc_ref, *refs))


def _nsa_attend_body(n_full, qa_ref, gate_ref, oc_ref, *refs):
    kv_refs, (tri_ref, band_ref, o_ref) = refs[:-3], refs[-3:]
    i = pl.program_id(1)
    q0 = i * Q_BLOCK
    lane = lax.broadcasted_iota(jnp.int32, (Q_BLOCK, LANES), 1)
    n_win = Q_BLOCK + WINDOW
    w0 = pl.multiple_of(jnp.maximum(q0 - WINDOW, 0), Q_BLOCK)
    tri = tri_ref[i % (SEL_CHUNK // Q_BLOCK)]
    tri = jnp.concatenate([tri, tri], axis=0)
    band = band_ref[jnp.minimum(i, band_ref.shape[0] - 1)]
    band = jnp.concatenate([band, band], axis=0)
    k_last = n_full * SEL_CHUNK
    for h in range(N_KV_HEADS):
        ks_ref, vs_ref, kw_ref, vw_ref = kv_refs[4 * h:4 * h + 4]
        gate = gate_ref[:, h * LANES:(h + 1) * LANES]
        for j in range(GROUP_SIZE // 2):
            heads = (2 * j, 2 * j + 1)
            pair = h * (GROUP_SIZE // 2) + j
            qa = jnp.concatenate(
                [qa_ref[:, (h * GROUP_SIZE + g) * LANES:(h * GROUP_SIZE + g + 1) * LANES] for g in heads], axis=0)
            s_w = _dot_nt(qa, kw_ref[pl.ds(w0, n_win), :]) + band
            o_w = _softmax2_av([s_w], [vw_ref[pl.ds(w0, n_win), :]])
            scores = [_dot_nt(qa, ks_ref[k_last:k_last + SEL_CHUNK, :]) + tri]
            values = [vs_ref[k_last:k_last + SEL_CHUNK, :]]
            if n_full:
                scores.append(_dot_nt(qa, ks_ref[0:k_last, :]))
                values.append(vs_ref[0:k_last, :])
            o_s = _softmax2_av(scores, values)

            def gate_col(r):
                return jnp.concatenate([gate[:, g * N_BRANCH + r:g * N_BRANCH + r + 1] for g in heads], axis=0)

            o = gate_col(1) * o_s + gate_col(2) * o_w
            cols = slice(pair * LANES, (pair + 1) * LANES)
            merged = oc_ref[:, cols].astype(F32) + _merge_head_pair(o[:Q_BLOCK], o[Q_BLOCK:], lane)
            o_ref[:, cols] = merged.astype(BF16)


def _nsa_attend(qa, gates, oc, kv8, tri, band, batch, seq):
    n_qb = seq // Q_BLOCK
    q_map = lambda b, i: (b * n_qb + i, 0)
    fixed3 = lambda b, i: (0, 0, 0)

    def kv_spec(slab):
        return pl.BlockSpec((seq, LANES), lambda b, i: (b, slab))

    kv_slabs = [branch * 2 * N_KV_HEADS + 2 * h + part
                for h in range(N_KV_HEADS) for branch in range(2) for part in range(2)]
    return pl.pallas_call(
        _nsa_attend_kernel,
        grid=(batch, n_qb),
        in_specs=[
            pl.BlockSpec((Q_BLOCK, N_HEADS * LANES), q_map),
            pl.BlockSpec((Q_BLOCK, N_KV_HEADS * LANES), q_map),
            pl.BlockSpec((Q_BLOCK, ATTN_WIDTH), q_map),
            *[kv_spec(s) for s in kv_slabs],
            pl.BlockSpec(tri.shape, fixed3),
            pl.BlockSpec(band.shape, fixed3),
        ],
        out_specs=pl.BlockSpec((Q_BLOCK, ATTN_WIDTH), q_map),
        out_shape=jax.ShapeDtypeStruct((batch * seq, ATTN_WIDTH), BF16),
        compiler_params=pltpu.CompilerParams(
            dimension_semantics=("arbitrary", "arbitrary"), vmem_limit_bytes=VMEM_LIMIT),
        name="nsa_attend",
    )(qa, gates, oc, *([kv8] * len(kv_slabs)), tri, band)


def _mix_ffn_ple_kernel(cx_ref, cxh_ref, bg_ref, y_ref, x_ref, p_ref, wc_ref, gc_ref, ga_ref, wo_ref, gf_ref,
                        wu_ref, cu_ref, wd_ref, gp_ref, wpg_ref, wpp_ref, gfin_ref,
                        out_ref, act_ref, n_prev_ref, *, tiles_per_seq):
    step = pl.program_id(0)
    seq_start = step % tiles_per_seq == 0

    @pl.when(step == 0)
    def _():
        n_prev_ref[...] = jnp.zeros_like(n_prev_ref)

    cx = cx_ref[...].astype(F32)
    s1, s2 = _shift_rows(cx, cxh_ref[...].astype(F32), seq_start)
    wc = wc_ref[...]
    y_conv = bg_ref[...].astype(F32) * (wc[0:1] * s2 + wc[1:2] * s1 + wc[2:3] * cx)
    mixed = jnp.concatenate(
        [_rms(y_conv, gc_ref[...]).astype(BF16), _rms(y_ref[...].astype(F32), ga_ref[...]).astype(BF16)], axis=1)
    h = x_ref[...] + _dot(mixed, wo_ref[...])

    n = _rms(h, gf_ref[...]).astype(BF16)
    nh = n_prev_ref[...]
    n_prev_ref[...] = n[n.shape[0] - BF16_ROWS:, :]

    def conv_up(lo):
        w = wu_ref[:, lo:lo + TF_FFN]
        a = _dot(n, w)
        t1, t2 = _shift_rows(a, _dot(nh, w), seq_start)
        c = cu_ref[:, lo:lo + TF_FFN]
        return c[0:1] * t2 + c[1:2] * t1 + c[2:3] * a

    for j in range(D_FF // TF_FFN):
        u_gate = conv_up(j * TF_FFN)
        u_val = conv_up(D_FF + j * TF_FFN)
        act_ref[:, j * TF_FFN:(j + 1) * TF_FFN] = (u_gate * _sigmoid(u_gate) * u_val).astype(BF16)
    h = h + _dot(act_ref[...], wd_ref[...])

    gate = _sigmoid(_dot(_rms(h, gp_ref[...]).astype(BF16), wpg_ref[...]))
    h = h + gate * _dot(p_ref[...].astype(BF16), wpp_ref[...])
    out_ref[...] = _rms(h, gfin_ref[...])


def _mix_ffn_ple(cx, bg, y_attn, x2, p2, w_conv, g_conv, g_attn, w_out, g_ffn, w_up, w_ffn_conv, w_down,
                 g_ple, w_gate, w_proj, g_final, seq):
    t = x2.shape[0]
    tm = TM_FFN
    row = lambda i: (i, 0)
    halo = lambda i: (jnp.maximum(i * (tm // BF16_ROWS) - 1, 0), 0)

    def resident(a):
        return pl.BlockSpec(a.shape, lambda i: (0, 0), pipeline_mode=pl.Buffered(1))

    weights = (w_conv, g_conv, g_attn, w_out, g_ffn, w_up, w_ffn_conv, w_down, g_ple, w_gate, w_proj, g_final)
    return pl.pallas_call(
        functools.partial(_mix_ffn_ple_kernel, tiles_per_seq=seq // tm),
        grid=(t // tm,),
        in_specs=[
            pl.BlockSpec((tm, CONV_WIDTH), row),
            pl.BlockSpec((BF16_ROWS, CONV_WIDTH), halo),
            pl.BlockSpec((tm, CONV_WIDTH), row),
            pl.BlockSpec((tm, ATTN_WIDTH), row),
            pl.BlockSpec((tm, D_MODEL), row),
            pl.BlockSpec((tm, PLE_DIM), row),
            *[resident(w) for w in weights],
        ],
        out_specs=pl.BlockSpec((tm, D_MODEL), row),
        out_shape=jax.ShapeDtypeStruct((t, D_MODEL), F32),
        scratch_shapes=[pltpu.VMEM((tm, D_FF), BF16), pltpu.VMEM((BF16_ROWS, D_MODEL), BF16)],
        compiler_params=pltpu.CompilerParams(
            dimension_semantics=("arbitrary",), vmem_limit_bytes=VMEM_LIMIT),
        name="mix_ffn_ple",
    )(cx, cx, bg, y_attn, x2, p2, *weights)


def _permute_in_proj(w_in):
    per_kv = GROUP_SIZE * N_BRANCH
    assert w_in.shape[1] == COL_GATE + N_KV_HEADS * per_kv
    pad = jnp.zeros((w_in.shape[0], LANES - per_kv), w_in.dtype)
    slabs = [w_in[:, :COL_GATE]]
    for h in range(N_KV_HEADS):
        slabs += [w_in[:, COL_GATE + h * per_kv:COL_GATE + (h + 1) * per_kv], pad]
    return jnp.concatenate(slabs, axis=1).astype(BF16)


def _rope_tables(pos):
    inv = ROPE_THETA ** (-jnp.arange(0, HEAD_DIM, 2, dtype=F32) / HEAD_DIM)
    ang = pos.astype(F32)[:, None] * inv[None, :]
    cos = jnp.concatenate([jnp.cos(ang), jnp.cos(ang)], axis=-1)
    sin = jnp.concatenate([-jnp.sin(ang), jnp.sin(ang)], axis=-1)
    return cos, sin


def _compress_weights(w1_k, w1_v, w2_k, w2_v, pe_k, pe_v):
    n_str = 2 * N_KV_HEADS
    half = CMP_LEN // 2
    w1 = jnp.stack([w1_k, w1_k, w1_v, w1_v]).reshape(n_str, 2, half, HEAD_DIM, HEAD_DIM)
    eye = jnp.eye(n_str, dtype=F32)
    w1_big = jnp.einsum('chrde,cx->rcdhxe', w1, eye).reshape(half * n_str * HEAD_DIM, 2 * n_str * HEAD_DIM)
    pe = jnp.stack([pe_k, pe_k, pe_v, pe_v]).reshape(n_str, 2, half, HEAD_DIM)
    pe_big = jnp.transpose(pe, (1, 2, 0, 3)).reshape(2, half * n_str * HEAD_DIM)
    pe_big = jnp.concatenate([pe_big, jnp.zeros((6, pe_big.shape[1]), F32)], axis=0)
    w2 = jnp.stack([w2_k, w2_k, w2_v, w2_v])
    out_pos = np.asarray([0, 4, 2, 6])
    place = jnp.zeros((n_str, 2 * n_str), F32).at[np.arange(n_str), out_pos].set(1.0)
    w2_big = jnp.einsum('cde,cx->cdxe', w2, place).reshape(n_str * HEAD_DIM, 2 * n_str * HEAD_DIM)
    return w1_big.astype(BF16), pe_big.astype(BF16), w2_big.astype(BF16)


def kernel(x, p, g_mix, w_in, w_conv_mix, cmp_pe_k, cmp_w1_k, cmp_w2_k, cmp_pe_v, cmp_w1_v, cmp_w2_v,
           g_gn_conv, g_gn_attn, w_out, g_ffn, w_up, w_ffn_conv, w_down, g_ple, w_ple_gate, w_ple_proj,
           g_final):
    batch, seq, _ = x.shape
    depth = w_in.shape[0]
    t = batch * seq
    n_cmp_pad = seq // CMP_STRIDE
    n_blk = seq // SEL_BLOCK

    cos, sin = _rope_tables(jnp.arange(seq, dtype=jnp.int32))
    cos_q = jnp.concatenate([cos, cos], axis=-1)
    sin_q = jnp.concatenate([sin, sin], axis=-1)
    cmp_start = jnp.arange(n_cmp_pad) * CMP_STRIDE
    cos_e, sin_e = _rope_tables(cmp_start + CMP_LEN - 1)
    ones = jnp.ones_like(cos_e)
    zeros = jnp.zeros_like(sin_e)
    reps = CMP_ROWS // n_cmp_pad
    cos_c = jnp.tile(jnp.concatenate([cos_e, ones, ones, ones] * N_KV_HEADS, axis=-1), (reps, 1))
    sin_c = jnp.tile(jnp.concatenate([sin_e, zeros, zeros, zeros] * N_KV_HEADS, axis=-1), (reps, 1))
    blk_start = jnp.arange(n_blk) * SEL_BLOCK
    n_cmp = (seq - CMP_LEN) // CMP_STRIDE + 1
    ovl = (jnp.clip(jnp.minimum(cmp_start[:, None] + CMP_LEN, blk_start[None, :] + SEL_BLOCK)
                    - jnp.maximum(cmp_start[:, None], blk_start[None, :]), 0, None).astype(F32) / CMP_LEN)
    ovl = jnp.where(jnp.arange(n_cmp_pad)[:, None] < n_cmp, ovl, 0.0)
    ovl_t = ovl.T.astype(BF16)
    lane = jnp.arange(LANES)[None, :]
    blk_hot = (lane - HEAD_DIM == jnp.arange(seq)[:, None] // SEL_BLOCK).astype(F32)
    qi = jnp.arange(Q_BLOCK)[None, :, None]
    n_tri = SEL_CHUNK // Q_BLOCK
    tri = jnp.where(jnp.arange(SEL_CHUNK)[None, None, :] <= jnp.arange(n_tri)[:, None, None] * Q_BLOCK + qi,
                    0.0, NEG).astype(F32)
    n_band = WINDOW // Q_BLOCK
    kk = jnp.arange(Q_BLOCK + WINDOW)[None, None, :]
    t_rel = jnp.minimum(jnp.arange(n_band + 1), n_band)[:, None, None] * Q_BLOCK + qi
    band = jnp.where((kk <= t_rel) & (kk > t_rel - WINDOW), 0.0, NEG).astype(F32)

    pad_rows = lambda w: jnp.concatenate([w, jnp.zeros((8 - w.shape[0], w.shape[1]), w.dtype)], axis=0)

    h = x.reshape(t, D_MODEL)
    for i in range(depth):
        w_perm = _permute_in_proj(w_in[i])
        cx, bg, q, kvc, kv8, gates = _inproj(h, g_mix[i][None, :], w_perm, cos_q, sin_q, blk_hot, seq)

        w1_big, pe_big, w2_big = _compress_weights(
            cmp_w1_k[i], cmp_w1_v[i], cmp_w2_k[i], cmp_w2_v[i], cmp_pe_k[i], cmp_pe_v[i])
        r = kvc.reshape(t // CMP_STRIDE, CMP_STRIDE * 2 * KV_WIDTH)
        kcv = _compress(r, w1_big, pe_big, w2_big, cos_c, sin_c)

        qa, oc = _nsa_select(q, gates, kcv, ovl_t, batch, seq)
        y_attn = _nsa_attend(qa, gates, oc, kv8, tri, band, batch, seq)

        assert depth == 1
        h = _mix_ffn_ple(cx, bg, y_attn, h, p[i].reshape(t, PLE_DIM), pad_rows(w_conv_mix[i]),
                         g_gn_conv[i][None, :], g_gn_attn[i][None, :], w_out[i].astype(BF16), g_ffn[i][None, :],
                         w_up[i].astype(BF16), pad_rows(w_ffn_conv[i]), w_down[i].astype(BF16),
                         g_ple[i][None, :], w_ple_gate[i].astype(BF16), w_ple_proj[i].astype(BF16),
                         g_final[None, :], seq)
    return h.reshape(batch, seq, D_MODEL)
```

```python
import functools

import numpy as np
import jax
import jax.numpy as jnp
from jax import lax
from jax.experimental import pallas as pl
from jax.experimental.pallas import tpu as pltpu

F32 = jnp.float32
BF16 = jnp.bfloat16

D_MODEL = 1024
PLE_DIM = 256
CONV_WIDTH = 512
N_HEADS = 8
N_KV_HEADS = 2
HEAD_DIM = 64
HALF_DIM = HEAD_DIM // 2
GROUP_SIZE = N_HEADS // N_KV_HEADS
ATTN_WIDTH = N_HEADS * HEAD_DIM
KV_WIDTH = N_KV_HEADS * HEAD_DIM
N_BRANCH = 3
CMP_LEN = 32
CMP_STRIDE = 16
SEL_BLOCK = 64
N_SEL = 8
WINDOW = 512
Q_BLOCK = 128
D_FF = 2816
ROPE_THETA = 10000.0
EPS = 1e-6
NEG = -1e30
SEL_FORCE = 1e4

LANES = 128
BF16_ROWS = 16
VMEM_LIMIT = 56 * 1024 * 1024

COL_XIN = 0
COL_B = CONV_WIDTH
COL_C = 2 * CONV_WIDTH
COL_Q = 3 * CONV_WIDTH
COL_KVC = COL_Q + ATTN_WIDTH
COL_KV4 = COL_KVC + 2 * KV_WIDTH
COL_GATE = COL_KV4 + 4 * KV_WIDTH
IN_COLS_PAD = COL_GATE + N_KV_HEADS * LANES
KV8_WIDTH = 2 * 2 * N_KV_HEADS * LANES
ONES_LANE = HEAD_DIM
Q_SCALE = HEAD_DIM ** -0.5 * float(np.log2(np.e))

TM_IN = 1024
TM_FFN = 512
TF_FFN = 256
CMP_ROWS = 512
SEL_CHUNK = 512
SEL_TILE = 512


def _dot(a, b):
    return jnp.dot(a, b, preferred_element_type=F32)


def _dot_nt(a, b):
    return lax.dot_general(a, b, (((1,), (1,)), ((), ())), preferred_element_type=F32)


def _sigmoid(x):
    return 1.0 / (1.0 + jnp.exp(-x))


def _rms(x, g):
    return x * lax.rsqrt(jnp.mean(x * x, axis=-1, keepdims=True) + EPS) * g


def _rope_slab(y, cos, sin_signed, first_half):
    fwd = pltpu.roll(y, HALF_DIM, 1)
    bwd = pltpu.roll(y, LANES - HALF_DIM, 1)
    return y * cos + jnp.where(first_half, bwd, fwd) * sin_signed


def _shift_rows(a, halo, seq_start):
    halo = jnp.where(seq_start, 0.0, halo)
    h1 = halo[BF16_ROWS - 1:BF16_ROWS, :]
    h2 = halo[BF16_ROWS - 2:BF16_ROWS - 1, :]
    row = lax.broadcasted_iota(jnp.int32, a.shape, 0)
    s1 = jnp.where(row == 0, h1, pltpu.roll(a, 1, 0))
    s2 = jnp.where(row == 0, h2, jnp.where(row == 1, h1, pltpu.roll(a, 2, 0)))
    return s1, s2


def _inproj_kernel(x_ref, g_ref, w_ref, cos_ref, sin_ref, hot_ref,
                   cx_ref, bg_ref, q_ref, kvc_ref, kv8_ref, gate_ref):
    n = _rms(x_ref[...], g_ref[...]).astype(BF16)

    def proj(lo, hi):
        return _dot(n, w_ref[:, lo:hi])

    x_in = proj(COL_XIN, COL_B)
    bg_ref[...] = proj(COL_B, COL_C).astype(BF16)
    cx_ref[...] = (proj(COL_C, COL_Q) * x_in).astype(BF16)

    cos = cos_ref[...]
    sin = sin_ref[...]
    lane = lax.broadcasted_iota(jnp.int32, cos.shape, 1)
    first_half = (lane % HEAD_DIM) < HALF_DIM
    for s in range(ATTN_WIDTH // LANES):
        y = proj(COL_Q + s * LANES, COL_Q + (s + 1) * LANES)
        q_ref[:, s * LANES:(s + 1) * LANES] = (_rope_slab(y, cos, sin, first_half) * Q_SCALE).astype(BF16)

    kvc_ref[...] = proj(COL_KVC, COL_KV4).astype(BF16)

    in_head = lane < HEAD_DIM
    ones_col = jnp.where(lane == ONES_LANE, 1.0, 0.0)
    zeros = jnp.zeros_like(cos)
    for branch, key_fill in enumerate((hot_ref[...], zeros)):
        k_pair = _rope_slab(proj(COL_KV4 + (2 * branch) * LANES, COL_KV4 + (2 * branch + 1) * LANES),
                            cos, sin, first_half)
        v_pair = proj(COL_KV4 + (2 * branch + 1) * LANES, COL_KV4 + (2 * branch + 2) * LANES)
        for h in range(N_KV_HEADS):
            k_h = k_pair if h == 0 else pltpu.roll(k_pair, HEAD_DIM, 1)
            v_h = v_pair if h == 0 else pltpu.roll(v_pair, HEAD_DIM, 1)
            slab = (branch * N_KV_HEADS + h) * 2
            kv8_ref[:, slab * LANES:(slab + 1) * LANES] = jnp.where(in_head, k_h, key_fill).astype(BF16)
            kv8_ref[:, (slab + 1) * LANES:(slab + 2) * LANES] = jnp.where(in_head, v_h, ones_col).astype(BF16)

    gate_ref[...] = _sigmoid(proj(COL_GATE, IN_COLS_PAD))


def _inproj(x2, g_mix, w_perm, cos_q, sin_q, blk_hot, seq):
    t = x2.shape[0]
    tm = TM_IN
    n_seq_tiles = seq // tm
    row = lambda i: (i, 0)
    fixed = lambda i: (0, 0)
    tab = lambda i: (i % n_seq_tiles, 0)
    return pl.pallas_call(
        _inproj_kernel,
        grid=(t // tm,),
        in_specs=[
            pl.BlockSpec((tm, D_MODEL), row),
            pl.BlockSpec((1, D_MODEL), fixed),
            pl.BlockSpec((D_MODEL, IN_COLS_PAD), fixed),
            pl.BlockSpec((tm, LANES), tab),
            pl.BlockSpec((tm, LANES), tab),
            pl.BlockSpec((tm, LANES), tab),
        ],
        out_specs=[
            pl.BlockSpec((tm, CONV_WIDTH), row),
            pl.BlockSpec((tm, CONV_WIDTH), row),
            pl.BlockSpec((tm, ATTN_WIDTH), row),
            pl.BlockSpec((tm, 2 * KV_WIDTH), row),
            pl.BlockSpec((tm, KV8_WIDTH), row),
            pl.BlockSpec((tm, N_KV_HEADS * LANES), row),
        ],
        out_shape=[
            jax.ShapeDtypeStruct((t, CONV_WIDTH), BF16),
            jax.ShapeDtypeStruct((t, CONV_WIDTH), BF16),
            jax.ShapeDtypeStruct((t, ATTN_WIDTH), BF16),
            jax.ShapeDtypeStruct((t, 2 * KV_WIDTH), BF16),
            jax.ShapeDtypeStruct((t, KV8_WIDTH), BF16),
            jax.ShapeDtypeStruct((t, N_KV_HEADS * LANES), F32),
        ],
        compiler_params=pltpu.CompilerParams(
            dimension_semantics=("arbitrary",), vmem_limit_bytes=VMEM_LIMIT),
        name="inproj",
    )(x2, g_mix, w_perm, cos_q, sin_q, blk_hot)


def _compress_kernel(r_ref, w1_ref, pe_ref, w2_ref, cos_ref, sin_ref, out_ref):
    width = 2 * KV_WIDTH
    p = _dot(r_ref[...], w1_ref[...])
    pb = _dot(pe_ref[...], w1_ref[...])
    bias = pb[0:1, :width] + pb[1:2, width:]
    rows = p.shape[0]
    nxt = pltpu.roll(p[:, width:], rows - 1, 0)
    pre = p[:, :width] + nxt + bias
    hid = (pre * _sigmoid(pre)).astype(BF16)
    y = _dot(hid, w2_ref[...])
    cos = cos_ref[...]
    sin = sin_ref[...]
    lane = lax.broadcasted_iota(jnp.int32, (rows, LANES), 1)
    first_half = (lane % HEAD_DIM) < HALF_DIM
    for s in range(y.shape[1] // LANES):
        sl = slice(s * LANES, (s + 1) * LANES)
        out_ref[:, sl] = _rope_slab(y[:, sl], cos[:, sl], sin[:, sl], first_half).astype(BF16)


def _compress(r, w1_big, pe_big, w2_big, cos_c, sin_c):
    rows = r.shape[0]
    width = w2_big.shape[1]
    row = lambda i: (i, 0)
    fixed = lambda i: (0, 0)
    return pl.pallas_call(
        _compress_kernel,
        grid=(rows // CMP_ROWS,),
        in_specs=[
            pl.BlockSpec((CMP_ROWS, r.shape[1]), row),
            pl.BlockSpec(w1_big.shape, fixed),
            pl.BlockSpec(pe_big.shape, fixed),
            pl.BlockSpec(w2_big.shape, fixed),
            pl.BlockSpec((CMP_ROWS, width), fixed),
            pl.BlockSpec((CMP_ROWS, width), fixed),
        ],
        out_specs=pl.BlockSpec((CMP_ROWS, width), row),
        out_shape=jax.ShapeDtypeStruct((rows, width), BF16),
        compiler_params=pltpu.CompilerParams(
            dimension_semantics=("arbitrary",), vmem_limit_bytes=VMEM_LIMIT),
        name="compress",
    )(r, w1_big, pe_big, w2_big, cos_c, sin_c)


def _masked_softmax2(s, mask):
    s = jnp.where(mask, s, NEG)
    m = jnp.max(s, axis=-1, keepdims=True)
    e = jnp.where(mask, jnp.exp2(s - m), 0.0)
    return e / jnp.maximum(jnp.sum(e, axis=-1, keepdims=True), 1e-30)


def _stack_heads(q, lane):
    parts = []
    for g in range(GROUP_SIZE):
        slab = q[:, (g // 2) * LANES:(g // 2 + 1) * LANES]
        if g % 2 == 1:
            slab = pltpu.roll(slab, HEAD_DIM, 1)
        parts.append(jnp.where(lane < HEAD_DIM, slab, 0.0))
    return parts


def _merge_head_pair(even, odd, lane):
    return jnp.where(lane < HEAD_DIM, even, pltpu.roll(odd, HEAD_DIM, 1))


def _nsa_select_kernel(q_ref, gate_ref, kc_ref, vc_ref, ovl_ref, qa_ref, oc_ref):
    tile = q_ref.shape[0]
    q0 = pl.program_id(2) * tile
    lane = lax.broadcasted_iota(jnp.int32, (tile, LANES), 1)
    parts = _stack_heads(q_ref[...].astype(F32), lane)
    qs = jnp.concatenate(parts, axis=0).astype(BF16)

    s_c = _dot_nt(qs, kc_ref[...])
    t_c = q0 + lax.broadcasted_iota(jnp.int32, s_c.shape, 0) % tile
    end_c = lax.broadcasted_iota(jnp.int32, s_c.shape, 1) * CMP_STRIDE + (CMP_LEN - 1)
    p_c = _masked_softmax2(s_c, end_c <= t_c)
    o_c = _dot(p_c.astype(BF16), vc_ref[...])

    psum = p_c[0:tile]
    for g in range(1, GROUP_SIZE):
        psum = psum + p_c[g * tile:(g + 1) * tile]
    p_hi = psum.astype(BF16)
    p_lo = (psum - p_hi.astype(F32)).astype(BF16)
    ovl = ovl_ref[...]
    imp = _dot_nt(ovl, p_hi) + _dot_nt(ovl, p_lo)
    n_blk = imp.shape[0]
    blk = lax.broadcasted_iota(jnp.int32, imp.shape, 0)
    t_q = q0 + lax.broadcasted_iota(jnp.int32, imp.shape, 1)
    cur = t_q // SEL_BLOCK
    valid = blk * SEL_BLOCK <= t_q
    forced = (blk == 0) | (blk == cur) | (blk == cur - 1)
    score = jnp.where(valid, imp + jnp.where(forced, SEL_FORCE, 0.0), NEG)
    rank = jnp.zeros(imp.shape, F32)
    for m in range(n_blk):
        other = score[m:m + 1, :]
        tie = jnp.where(blk > m, 1.0, 0.0)
        rank = rank + jnp.where(other > score, 1.0, 0.0) + jnp.where(other == score, tie, 0.0)
    sel_neg = jnp.where((rank < float(min(N_SEL, n_blk))) & valid, 0.0, NEG)
    pad_lo = jnp.zeros((HEAD_DIM, tile), F32)
    pad_hi = jnp.zeros((LANES - HEAD_DIM - n_blk, tile), F32)
    sel_rows = jnp.concatenate([pad_lo, sel_neg, pad_hi], axis=0).T
    for g in range(GROUP_SIZE):
        qa_ref[:, g * LANES:(g + 1) * LANES] = (parts[g] + sel_rows).astype(BF16)

    gate = gate_ref[...]
    for j in range(GROUP_SIZE // 2):
        pair = [gate[:, g * N_BRANCH:g * N_BRANCH + 1] * o_c[g * tile:(g + 1) * tile] for g in (2 * j, 2 * j + 1)]
        oc_ref[:, j * LANES:(j + 1) * LANES] = _merge_head_pair(pair[0], pair[1], lane).astype(BF16)


def _nsa_select(q, gates, kcv, ovl_t, batch, seq):
    tile = SEL_TILE
    n_t = seq // tile
    n_cmp_pad = kcv.shape[0] // batch
    grp = GROUP_SIZE * HEAD_DIM
    q_map = lambda b, h, i: (b * n_t + i, h)
    return pl.pallas_call(
        _nsa_select_kernel,
        grid=(batch, N_KV_HEADS, n_t),
        in_specs=[
            pl.BlockSpec((tile, grp), q_map),
            pl.BlockSpec((tile, LANES), q_map),
            pl.BlockSpec((n_cmp_pad, LANES), lambda b, h, i: (b, 2 * h)),
            pl.BlockSpec((n_cmp_pad, LANES), lambda b, h, i: (b, 2 * h + 1)),
            pl.BlockSpec(ovl_t.shape, lambda b, h, i: (0, 0)),
        ],
        out_specs=[pl.BlockSpec((tile, GROUP_SIZE * LANES), q_map), pl.BlockSpec((tile, grp), q_map)],
        out_shape=[jax.ShapeDtypeStruct((batch * seq, N_HEADS * LANES), BF16),
                   jax.ShapeDtypeStruct((batch * seq, ATTN_WIDTH), BF16)],
        compiler_params=pltpu.CompilerParams(
            dimension_semantics=("arbitrary", "arbitrary", "arbitrary"), vmem_limit_bytes=VMEM_LIMIT),
        name="nsa_select",
    )(q, gates, kcv, kcv, ovl_t)


def _nsa_attend_kernel(qa_ref, gate_ref, oc_ref, *refs):
    i = pl.program_id(1)
    seq = refs[0].shape[0]
    for n_full in range(seq // SEL_CHUNK):
        pl.when(i // (SEL_CHUNK // Q_BLOCK) == n_full)(functools.partial(
            _nsa_attend_body, n_full, qa_ref, gate_ref, oc_ref, *refs))


def _nsa_attend_body(n_full, qa_ref, gate_ref, oc_ref, *refs):
    kv_refs, (tri_ref, band_ref, o_ref) = refs[:-3], refs[-3:]
    i = pl.program_id(1)
    q0 = i * Q_BLOCK
    lane = lax.broadcasted_iota(jnp.int32, (Q_BLOCK, LANES), 1)
    n_win = Q_BLOCK + WINDOW
    w0 = pl.multiple_of(jnp.maximum(q0 - WINDOW, 0), Q_BLOCK)
    tri = tri_ref[i % (SEL_CHUNK // Q_BLOCK)]
    tri = jnp.concatenate([tri, tri], axis=0)
    band = band_ref[jnp.minimum(i, band_ref.shape[0] - 1)]
    band = jnp.concatenate([band, band], axis=0)
    k_last = n_full * SEL_CHUNK

    def queries(h, j):
        return jnp.concatenate(
            [qa_ref[:, (h * GROUP_SIZE + g) * LANES:(h * GROUP_SIZE + g + 1) * LANES] for g in (2 * j, 2 * j + 1)],
            axis=0)

    def score_mats(branch, h, j):
        ks_ref, _, kw_ref, _ = kv_refs[4 * h:4 * h + 4]
        if branch == "win":
            return [_dot_nt(queries(h, j), kw_ref[pl.ds(w0, n_win), :]) + band]
        scores = [_dot_nt(queries(h, j), ks_ref[k_last:k_last + SEL_CHUNK, :]) + tri]
        if n_full:
            scores.append(_dot_nt(queries(h, j), ks_ref[0:k_last, :]))
        return scores

    def numerators(scores):
        m = functools.reduce(jnp.maximum, [jnp.max(s, axis=-1, keepdims=True) for s in scores])
        return [jnp.exp2(s - m).astype(BF16) for s in scores]

    def weighted(branch, h, probs):
        _, vs_ref, _, vw_ref = kv_refs[4 * h:4 * h + 4]
        if branch == "win":
            acc = _dot(probs[0], vw_ref[pl.ds(w0, n_win), :])
        else:
            acc = _dot(probs[0], vs_ref[k_last:k_last + SEL_CHUNK, :])
            if n_full:
                acc = acc + _dot(probs[1], vs_ref[0:k_last, :])
        return acc / jnp.maximum(acc[:, ONES_LANE:ONES_LANE + 1], 1e-30)

    def merge(h, j, o_w, o_s):
        gate = gate_ref[:, h * LANES:(h + 1) * LANES]

        def gate_col(r):
            return jnp.concatenate(
                [gate[:, g * N_BRANCH + r:g * N_BRANCH + r + 1] for g in (2 * j, 2 * j + 1)], axis=0)

        o = gate_col(1) * o_s + gate_col(2) * o_w
        cols = slice((h * (GROUP_SIZE // 2) + j) * LANES, (h * (GROUP_SIZE // 2) + j + 1) * LANES)
        merged = oc_ref[:, cols].astype(F32) + _merge_head_pair(o[:Q_BLOCK], o[Q_BLOCK:], lane)
        o_ref[:, cols] = merged.astype(BF16)

    chains = [(branch, h, j) for h in range(N_KV_HEADS) for j in range(GROUP_SIZE // 2)
              for branch in ("win", "sel")]
    scores, probs, outs = {}, {}, {}
    for step in range(len(chains) + 2):
        if step < len(chains):
            scores[step] = score_mats(*chains[step])
        if 0 <= step - 1 < len(chains):
            probs[step - 1] = numerators(scores.pop(step - 1))
        if 0 <= step - 2 < len(chains):
            branch, h, j = chains[step - 2]
            outs[branch] = weighted(branch, h, probs.pop(step - 2))
            if branch == "sel":
                merge(h, j, outs.pop("win"), outs.pop("sel"))


def _nsa_attend(qa, gates, oc, kv8, tri, band, batch, seq):
    n_qb = seq // Q_BLOCK
    q_map = lambda b, i: (b * n_qb + i, 0)
    fixed3 = lambda b, i: (0, 0, 0)

    def kv_spec(slab):
        return pl.BlockSpec((seq, LANES), lambda b, i: (b, slab))

    kv_slabs = [branch * 2 * N_KV_HEADS + 2 * h + part
                for h in range(N_KV_HEADS) for branch in range(2) for part in range(2)]
    return pl.pallas_call(
        _nsa_attend_kernel,
        grid=(batch, n_qb),
        in_specs=[
            pl.BlockSpec((Q_BLOCK, N_HEADS * LANES), q_map),
            pl.BlockSpec((Q_BLOCK, N_KV_HEADS * LANES), q_map),
            pl.BlockSpec((Q_BLOCK, ATTN_WIDTH), q_map),
            *[kv_spec(s) for s in kv_slabs],
            pl.BlockSpec(tri.shape, fixed3),
            pl.BlockSpec(band.shape, fixed3),
        ],
        out_specs=pl.BlockSpec((Q_BLOCK, ATTN_WIDTH), q_map),
        out_shape=jax.ShapeDtypeStruct((batch * seq, ATTN_WIDTH), BF16),
        compiler_params=pltpu.CompilerParams(
            dimension_semantics=("arbitrary", "arbitrary"), vmem_limit_bytes=VMEM_LIMIT),
        name="nsa_attend",
    )(qa, gates, oc, *([kv8] * len(kv_slabs)), tri, band)


def _mix_ffn_ple_kernel(cx_ref, cxh_ref, bg_ref, y_ref, x_ref, p_ref, wc_ref, gc_ref, ga_ref, wo_ref, gf_ref,
                        wu_ref, cu_ref, wd_ref, gp_ref, wpg_ref, wpp_ref, gfin_ref,
                        out_ref, act_ref, n_prev_ref, *, tiles_per_seq):
    step = pl.program_id(0)
    seq_start = step % tiles_per_seq == 0

    @pl.when(step == 0)
    def _():
        n_prev_ref[...] = jnp.zeros_like(n_prev_ref)

    cx = cx_ref[...].astype(F32)
    s1, s2 = _shift_rows(cx, cxh_ref[...].astype(F32), seq_start)
    wc = wc_ref[...]
    y_conv = bg_ref[...].astype(F32) * (wc[0:1] * s2 + wc[1:2] * s1 + wc[2:3] * cx)
    mixed = jnp.concatenate(
        [_rms(y_conv, gc_ref[...]).astype(BF16), _rms(y_ref[...].astype(F32), ga_ref[...]).astype(BF16)], axis=1)
    h = x_ref[...] + _dot(mixed, wo_ref[...])

    n = _rms(h, gf_ref[...]).astype(BF16)
    nh = n_prev_ref[...]
    n_prev_ref[...] = n[n.shape[0] - BF16_ROWS:, :]

    def conv_up(lo):
        w = wu_ref[:, lo:lo + TF_FFN]
        a = _dot(n, w)
        t1, t2 = _shift_rows(a, _dot(nh, w), seq_start)
        c = cu_ref[:, lo:lo + TF_FFN]
        return c[0:1] * t2 + c[1:2] * t1 + c[2:3] * a

    for j in range(D_FF // TF_FFN):
        u_gate = conv_up(j * TF_FFN)
        u_val = conv_up(D_FF + j * TF_FFN)
        act_ref[:, j * TF_FFN:(j + 1) * TF_FFN] = (u_gate * _sigmoid(u_gate) * u_val).astype(BF16)
    h = h + _dot(act_ref[...], wd_ref[...])

    gate = _sigmoid(_dot(_rms(h, gp_ref[...]).astype(BF16), wpg_ref[...]))
    h = h + gate * _dot(p_ref[...].astype(BF16), wpp_ref[...])
    out_ref[...] = _rms(h, gfin_ref[...])


def _mix_ffn_ple(cx, bg, y_attn, x2, p2, w_conv, g_conv, g_attn, w_out, g_ffn, w_up, w_ffn_conv, w_down,
                 g_ple, w_gate, w_proj, g_final, seq):
    t = x2.shape[0]
    tm = TM_FFN
    row = lambda i: (i, 0)
    halo = lambda i: (jnp.maximum(i * (tm // BF16_ROWS) - 1, 0), 0)

    def resident(a):
        return pl.BlockSpec(a.shape, lambda i: (0, 0), pipeline_mode=pl.Buffered(1))

    weights = (w_conv, g_conv, g_attn, w_out, g_ffn, w_up, w_ffn_conv, w_down, g_ple, w_gate, w_proj, g_final)
    return pl.pallas_call(
        functools.partial(_mix_ffn_ple_kernel, tiles_per_seq=seq // tm),
        grid=(t // tm,),
        in_specs=[
            pl.BlockSpec((tm, CONV_WIDTH), row),
            pl.BlockSpec((BF16_ROWS, CONV_WIDTH), halo),
            pl.BlockSpec((tm, CONV_WIDTH), row),
            pl.BlockSpec((tm, ATTN_WIDTH), row),
            pl.BlockSpec((tm, D_MODEL), row),
            pl.BlockSpec((tm, PLE_DIM), row),
            *[resident(w) for w in weights],
        ],
        out_specs=pl.BlockSpec((tm, D_MODEL), row),
        out_shape=jax.ShapeDtypeStruct((t, D_MODEL), F32),
        scratch_shapes=[pltpu.VMEM((tm, D_FF), BF16), pltpu.VMEM((BF16_ROWS, D_MODEL), BF16)],
        compiler_params=pltpu.CompilerParams(
            dimension_semantics=("arbitrary",), vmem_limit_bytes=VMEM_LIMIT),
        name="mix_ffn_ple",
    )(cx, cx, bg, y_attn, x2, p2, *weights)


def _permute_in_proj(w_in):
    per_kv = GROUP_SIZE * N_BRANCH
    assert w_in.shape[1] == COL_GATE + N_KV_HEADS * per_kv
    pad = jnp.zeros((w_in.shape[0], LANES - per_kv), w_in.dtype)
    slabs = [w_in[:, :COL_GATE]]
    for h in range(N_KV_HEADS):
        slabs += [w_in[:, COL_GATE + h * per_kv:COL_GATE + (h + 1) * per_kv], pad]
    return jnp.concatenate(slabs, axis=1).astype(BF16)


def _rope_tables(pos):
    inv = ROPE_THETA ** (-jnp.arange(0, HEAD_DIM, 2, dtype=F32) / HEAD_DIM)
    ang = pos.astype(F32)[:, None] * inv[None, :]
    cos = jnp.concatenate([jnp.cos(ang), jnp.cos(ang)], axis=-1)
    sin = jnp.concatenate([-jnp.sin(ang), jnp.sin(ang)], axis=-1)
    return cos, sin


def _compress_weights(w1_k, w1_v, w2_k, w2_v, pe_k, pe_v):
    n_str = 2 * N_KV_HEADS
    half = CMP_LEN // 2
    w1 = jnp.stack([w1_k, w1_k, w1_v, w1_v]).reshape(n_str, 2, half, HEAD_DIM, HEAD_DIM)
    eye = jnp.eye(n_str, dtype=F32)
    w1_big = jnp.einsum('chrde,cx->rcdhxe', w1, eye).reshape(half * n_str * HEAD_DIM, 2 * n_str * HEAD_DIM)
    pe = jnp.stack([pe_k, pe_k, pe_v, pe_v]).reshape(n_str, 2, half, HEAD_DIM)
    pe_big = jnp.transpose(pe, (1, 2, 0, 3)).reshape(2, half * n_str * HEAD_DIM)
    pe_big = jnp.concatenate([pe_big, jnp.zeros((6, pe_big.shape[1]), F32)], axis=0)
    w2 = jnp.stack([w2_k, w2_k, w2_v, w2_v])
    out_pos = np.asarray([0, 4, 2, 6])
    place = jnp.zeros((n_str, 2 * n_str), F32).at[np.arange(n_str), out_pos].set(1.0)
    w2_big = jnp.einsum('cde,cx->cdxe', w2, place).reshape(n_str * HEAD_DIM, 2 * n_str * HEAD_DIM)
    return w1_big.astype(BF16), pe_big.astype(BF16), w2_big.astype(BF16)


def kernel(x, p, g_mix, w_in, w_conv_mix, cmp_pe_k, cmp_w1_k, cmp_w2_k, cmp_pe_v, cmp_w1_v, cmp_w2_v,
           g_gn_conv, g_gn_attn, w_out, g_ffn, w_up, w_ffn_conv, w_down, g_ple, w_ple_gate, w_ple_proj,
           g_final):
    batch, seq, _ = x.shape
    depth = w_in.shape[0]
    t = batch * seq
    n_cmp_pad = seq // CMP_STRIDE
    n_blk = seq // SEL_BLOCK

    cos, sin = _rope_tables(jnp.arange(seq, dtype=jnp.int32))
    cos_q = jnp.concatenate([cos, cos], axis=-1)
    sin_q = jnp.concatenate([sin, sin], axis=-1)
    cmp_start = jnp.arange(n_cmp_pad) * CMP_STRIDE
    cos_e, sin_e = _rope_tables(cmp_start + CMP_LEN - 1)
    ones = jnp.ones_like(cos_e)
    zeros = jnp.zeros_like(sin_e)
    reps = CMP_ROWS // n_cmp_pad
    cos_c = jnp.tile(jnp.concatenate([cos_e, ones, ones, ones] * N_KV_HEADS, axis=-1), (reps, 1))
    sin_c = jnp.tile(jnp.concatenate([sin_e, zeros, zeros, zeros] * N_KV_HEADS, axis=-1), (reps, 1))
    blk_start = jnp.arange(n_blk) * SEL_BLOCK
    n_cmp = (seq - CMP_LEN) // CMP_STRIDE + 1
    ovl = (jnp.clip(jnp.minimum(cmp_start[:, None] + CMP_LEN, blk_start[None, :] + SEL_BLOCK)
                    - jnp.maximum(cmp_start[:, None], blk_start[None, :]), 0, None).astype(F32) / CMP_LEN)
    ovl = jnp.where(jnp.arange(n_cmp_pad)[:, None] < n_cmp, ovl, 0.0)
    ovl_t = ovl.T.astype(BF16)
    lane = jnp.arange(LANES)[None, :]
    blk_hot = (lane - HEAD_DIM == jnp.arange(seq)[:, None] // SEL_BLOCK).astype(F32)
    qi = jnp.arange(Q_BLOCK)[None, :, None]
    n_tri = SEL_CHUNK // Q_BLOCK
    tri = jnp.where(jnp.arange(SEL_CHUNK)[None, None, :] <= jnp.arange(n_tri)[:, None, None] * Q_BLOCK + qi,
                    0.0, NEG).astype(F32)
    n_band = WINDOW // Q_BLOCK
    kk = jnp.arange(Q_BLOCK + WINDOW)[None, None, :]
    t_rel = jnp.minimum(jnp.arange(n_band + 1), n_band)[:, None, None] * Q_BLOCK + qi
    band = jnp.where((kk <= t_rel) & (kk > t_rel - WINDOW), 0.0, NEG).astype(F32)

    pad_rows = lambda w: jnp.concatenate([w, jnp.zeros((8 - w.shape[0], w.shape[1]), w.dtype)], axis=0)

    h = x.reshape(t, D_MODEL)
    for i in range(depth):
        w_perm = _permute_in_proj(w_in[i])
        cx, bg, q, kvc, kv8, gates = _inproj(h, g_mix[i][None, :], w_perm, cos_q, sin_q, blk_hot, seq)

        w1_big, pe_big, w2_big = _compress_weights(
            cmp_w1_k[i], cmp_w1_v[i], cmp_w2_k[i], cmp_w2_v[i], cmp_pe_k[i], cmp_pe_v[i])
        r = kvc.reshape(t // CMP_STRIDE, CMP_STRIDE * 2 * KV_WIDTH)
        kcv = _compress(r, w1_big, pe_big, w2_big, cos_c, sin_c)

        qa, oc = _nsa_select(q, gates, kcv, ovl_t, batch, seq)
        y_attn = _nsa_attend(qa, gates, oc, kv8, tri, band, batch, seq)

        assert depth == 1
        h = _mix_ffn_ple(cx, bg, y_attn, h, p[i].reshape(t, PLE_DIM), pad_rows(w_conv_mix[i]),
                         g_gn_conv[i][None, :], g_gn_attn[i][None, :], w_out[i].astype(BF16), g_ffn[i][None, :],
                         w_up[i].astype(BF16), pad_rows(w_ffn_conv[i]), w_down[i].astype(BF16),
                         g_ple[i][None, :], w_ple_gate[i].astype(BF16), w_ple_proj[i].astype(BF16),
                         g_final[None, :], seq)
    return h.reshape(batch, seq, D_MODEL)
```

```python
import functools

import numpy as np
import jax
import jax.numpy as jnp
from jax import lax
from jax.experimental import pallas as pl
from jax.experimental.pallas import tpu as pltpu

F32 = jnp.float32
BF16 = jnp.bfloat16

D_MODEL = 1024
PLE_DIM = 256
CONV_WIDTH = 512
N_HEADS = 8
N_KV_HEADS = 2
HEAD_DIM = 64
HALF_DIM = HEAD_DIM // 2
GROUP_SIZE = N_HEADS // N_KV_HEADS
ATTN_WIDTH = N_HEADS * HEAD_DIM
KV_WIDTH = N_KV_HEADS * HEAD_DIM
N_BRANCH = 3
CMP_LEN = 32
CMP_STRIDE = 16
SEL_BLOCK = 64
N_SEL = 8
WINDOW = 512
Q_BLOCK = 128
D_FF = 2816
ROPE_THETA = 10000.0
EPS = 1e-6
NEG = -1e30
SEL_FORCE = 1e4

LANES = 128
BF16_ROWS = 16
VMEM_LIMIT = 56 * 1024 * 1024

COL_XIN = 0
COL_B = CONV_WIDTH
COL_C = 2 * CONV_WIDTH
COL_Q = 3 * CONV_WIDTH
COL_KVC = COL_Q + ATTN_WIDTH
COL_KV4 = COL_KVC + 2 * KV_WIDTH
COL_GATE = COL_KV4 + 4 * KV_WIDTH
IN_COLS_PAD = COL_GATE + N_KV_HEADS * LANES
KV8_WIDTH = 2 * 2 * N_KV_HEADS * LANES
Q_SCALE = HEAD_DIM ** -0.5 * float(np.log2(np.e))

TM_IN = 1024
TM_FFN = 512
TF_FFN = 256
CMP_ROWS = 512
SEL_TILE = 512


def _dot(a, b):
    return jnp.dot(a, b, preferred_element_type=F32)


def _dot_nt(a, b):
    return lax.dot_general(a, b, (((1,), (1,)), ((), ())), preferred_element_type=F32)


def _sigmoid(x):
    return 1.0 / (1.0 + jnp.exp(-x))


def _rms(x, g):
    return x * lax.rsqrt(jnp.mean(x * x, axis=-1, keepdims=True) + EPS) * g


def _rope_slab(y, cos, sin_signed, first_half):
    fwd = pltpu.roll(y, HALF_DIM, 1)
    bwd = pltpu.roll(y, LANES - HALF_DIM, 1)
    return y * cos + jnp.where(first_half, bwd, fwd) * sin_signed


def _shift_rows(a, halo, seq_start):
    halo = jnp.where(seq_start, 0.0, halo)
    h1 = halo[BF16_ROWS - 1:BF16_ROWS, :]
    h2 = halo[BF16_ROWS - 2:BF16_ROWS - 1, :]
    row = lax.broadcasted_iota(jnp.int32, a.shape, 0)
    s1 = jnp.where(row == 0, h1, pltpu.roll(a, 1, 0))
    s2 = jnp.where(row == 0, h2, jnp.where(row == 1, h1, pltpu.roll(a, 2, 0)))
    return s1, s2


def _inproj_kernel(x_ref, g_ref, w_ref, cos_ref, sin_ref, hot_ref,
                   cx_ref, bg_ref, q_ref, kvc_ref, kv8_ref, gate_ref):
    n = _rms(x_ref[...], g_ref[...]).astype(BF16)

    def proj(lo, hi):
        return _dot(n, w_ref[:, lo:hi])

    x_in = proj(COL_XIN, COL_B)
    bg_ref[...] = proj(COL_B, COL_C).astype(BF16)
    cx_ref[...] = (proj(COL_C, COL_Q) * x_in).astype(BF16)

    cos = cos_ref[...]
    sin = sin_ref[...]
    lane = lax.broadcasted_iota(jnp.int32, cos.shape, 1)
    first_half = (lane % HEAD_DIM) < HALF_DIM
    for s in range(ATTN_WIDTH // LANES):
        y = proj(COL_Q + s * LANES, COL_Q + (s + 1) * LANES)
        q_ref[:, s * LANES:(s + 1) * LANES] = (_rope_slab(y, cos, sin, first_half) * Q_SCALE).astype(BF16)

    kvc_ref[...] = proj(COL_KVC, COL_KV4).astype(BF16)

    in_head = lane < HEAD_DIM
    ones_col = jnp.where(in_head, 0.0, 1.0)
    zeros = jnp.zeros_like(cos)
    for branch, key_fill in enumerate((hot_ref[...], zeros)):
        k_pair = _rope_slab(proj(COL_KV4 + (2 * branch) * LANES, COL_KV4 + (2 * branch + 1) * LANES),
                            cos, sin, first_half)
        v_pair = proj(COL_KV4 + (2 * branch + 1) * LANES, COL_KV4 + (2 * branch + 2) * LANES)
        for h in range(N_KV_HEADS):
            k_h = k_pair if h == 0 else pltpu.roll(k_pair, HEAD_DIM, 1)
            v_h = v_pair if h == 0 else pltpu.roll(v_pair, HEAD_DIM, 1)
            slab = (branch * N_KV_HEADS + h) * 2
            kv8_ref[:, slab * LANES:(slab + 1) * LANES] = jnp.where(in_head, k_h, key_fill).astype(BF16)
            kv8_ref[:, (slab + 1) * LANES:(slab + 2) * LANES] = jnp.where(in_head, v_h, ones_col).astype(BF16)

    gate_ref[...] = _sigmoid(proj(COL_GATE, IN_COLS_PAD))


def _inproj(x2, g_mix, w_perm, cos_q, sin_q, blk_hot, seq):
    t = x2.shape[0]
    tm = TM_IN
    n_seq_tiles = seq // tm
    row = lambda i: (i, 0)
    fixed = lambda i: (0, 0)
    tab = lambda i: (i % n_seq_tiles, 0)
    return pl.pallas_call(
        _inproj_kernel,
        grid=(t // tm,),
        in_specs=[
            pl.BlockSpec((tm, D_MODEL), row),
            pl.BlockSpec((1, D_MODEL), fixed),
            pl.BlockSpec((D_MODEL, IN_COLS_PAD), fixed),
            pl.BlockSpec((tm, LANES), tab),
            pl.BlockSpec((tm, LANES), tab),
            pl.BlockSpec((tm, LANES), tab),
        ],
        out_specs=[
            pl.BlockSpec((tm, CONV_WIDTH), row),
            pl.BlockSpec((tm, CONV_WIDTH), row),
            pl.BlockSpec((tm, ATTN_WIDTH), row),
            pl.BlockSpec((tm, 2 * KV_WIDTH), row),
            pl.BlockSpec((tm, KV8_WIDTH), row),
            pl.BlockSpec((tm, N_KV_HEADS * LANES), row),
        ],
        out_shape=[
            jax.ShapeDtypeStruct((t, CONV_WIDTH), BF16),
            jax.ShapeDtypeStruct((t, CONV_WIDTH), BF16),
            jax.ShapeDtypeStruct((t, ATTN_WIDTH), BF16),
            jax.ShapeDtypeStruct((t, 2 * KV_WIDTH), BF16),
            jax.ShapeDtypeStruct((t, KV8_WIDTH), BF16),
            jax.ShapeDtypeStruct((t, N_KV_HEADS * LANES), F32),
        ],
        compiler_params=pltpu.CompilerParams(
            dimension_semantics=("arbitrary",), vmem_limit_bytes=VMEM_LIMIT),
        name="inproj",
    )(x2, g_mix, w_perm, cos_q, sin_q, blk_hot)


def _compress_kernel(r_ref, w1_ref, pe_ref, w2_ref, cos_ref, sin_ref, out_ref):
    width = 2 * KV_WIDTH
    p = _dot(r_ref[...], w1_ref[...])
    pb = _dot(pe_ref[...], w1_ref[...])
    bias = pb[0:1, :width] + pb[1:2, width:]
    rows = p.shape[0]
    nxt = pltpu.roll(p[:, width:], rows - 1, 0)
    pre = p[:, :width] + nxt + bias
    hid = (pre * _sigmoid(pre)).astype(BF16)
    y = _dot(hid, w2_ref[...])
    cos = cos_ref[...]
    sin = sin_ref[...]
    lane = lax.broadcasted_iota(jnp.int32, (rows, LANES), 1)
    first_half = (lane % HEAD_DIM) < HALF_DIM
    for s in range(y.shape[1] // LANES):
        sl = slice(s * LANES, (s + 1) * LANES)
        out_ref[:, sl] = _rope_slab(y[:, sl], cos[:, sl], sin[:, sl], first_half).astype(BF16)


def _compress(r, w1_big, pe_big, w2_big, cos_c, sin_c):
    rows = r.shape[0]
    width = w2_big.shape[1]
    row = lambda i: (i, 0)
    fixed = lambda i: (0, 0)
    return pl.pallas_call(
        _compress_kernel,
        grid=(rows // CMP_ROWS,),
        in_specs=[
            pl.BlockSpec((CMP_ROWS, r.shape[1]), row),
            pl.BlockSpec(w1_big.shape, fixed),
            pl.BlockSpec(pe_big.shape, fixed),
            pl.BlockSpec(w2_big.shape, fixed),
            pl.BlockSpec((CMP_ROWS, width), fixed),
            pl.BlockSpec((CMP_ROWS, width), fixed),
        ],
        out_specs=pl.BlockSpec((CMP_ROWS, width), row),
        out_shape=jax.ShapeDtypeStruct((rows, width), BF16),
        compiler_params=pltpu.CompilerParams(
            dimension_semantics=("arbitrary",), vmem_limit_bytes=VMEM_LIMIT),
        name="compress",
    )(r, w1_big, pe_big, w2_big, cos_c, sin_c)


def _masked_softmax2(s, mask):
    s = jnp.where(mask, s, NEG)
    m = jnp.max(s, axis=-1, keepdims=True)
    e = jnp.where(mask, jnp.exp2(s - m), 0.0)
    return e / jnp.maximum(jnp.sum(e, axis=-1, keepdims=True), 1e-30)


def _stack_heads(q, lane):
    parts = []
    for g in range(GROUP_SIZE):
        slab = q[:, (g // 2) * LANES:(g // 2 + 1) * LANES]
        if g % 2 == 1:
            slab = pltpu.roll(slab, HEAD_DIM, 1)
        parts.append(jnp.where(lane < HEAD_DIM, slab, 0.0))
    return parts


def _merge_head_pair(even, odd, lane):
    return jnp.where(lane < HEAD_DIM, even, pltpu.roll(odd, HEAD_DIM, 1))


def _expand_gates(gate, expand):
    hi = gate.astype(BF16)
    lo = (gate - hi.astype(F32)).astype(BF16)
    return _dot(hi, expand) + _dot(lo, expand)


def _nsa_select_kernel(q_ref, gate_ref, kc_ref, vc_ref, ovl_ref, gx_ref, qa_ref, oc_ref):
    tile = q_ref.shape[0]
    q0 = pl.program_id(2) * tile
    lane = lax.broadcasted_iota(jnp.int32, (tile, LANES), 1)
    parts = _stack_heads(q_ref[...].astype(F32), lane)
    qs = jnp.concatenate(parts, axis=0).astype(BF16)

    s_c = _dot_nt(qs, kc_ref[...])
    t_c = q0 + lax.broadcasted_iota(jnp.int32, s_c.shape, 0) % tile
    end_c = lax.broadcasted_iota(jnp.int32, s_c.shape, 1) * CMP_STRIDE + (CMP_LEN - 1)
    p_c = _masked_softmax2(s_c, end_c <= t_c)
    o_c = _dot(p_c.astype(BF16), vc_ref[...])

    psum = p_c[0:tile]
    for g in range(1, GROUP_SIZE):
        psum = psum + p_c[g * tile:(g + 1) * tile]
    p_hi = psum.astype(BF16)
    p_lo = (psum - p_hi.astype(F32)).astype(BF16)
    ovl = ovl_ref[...]
    imp = _dot_nt(ovl, p_hi) + _dot_nt(ovl, p_lo)
    n_blk = imp.shape[0]
    blk = lax.broadcasted_iota(jnp.int32, imp.shape, 0)
    t_q = q0 + lax.broadcasted_iota(jnp.int32, imp.shape, 1)
    cur = t_q // SEL_BLOCK
    valid = blk * SEL_BLOCK <= t_q
    forced = (blk == 0) | (blk == cur) | (blk == cur - 1)
    score = jnp.where(valid, imp + jnp.where(forced, SEL_FORCE, 0.0), NEG)
    rank = jnp.zeros(imp.shape, F32)
    for m in range(n_blk):
        other = score[m:m + 1, :]
        tie = jnp.where(blk > m, 1.0, 0.0)
        rank = rank + jnp.where(other > score, 1.0, 0.0) + jnp.where(other == score, tie, 0.0)
    sel_neg = jnp.where((rank < float(min(N_SEL, n_blk))) & valid, 0.0, NEG)
    pad_lo = jnp.zeros((HEAD_DIM, tile), F32)
    pad_hi = jnp.zeros((LANES - HEAD_DIM - n_blk, tile), F32)
    sel_rows = jnp.concatenate([pad_lo, sel_neg, pad_hi], axis=0).T
    for g in range(GROUP_SIZE):
        qa_ref[:, g * LANES:(g + 1) * LANES] = (parts[g] + sel_rows).astype(BF16)

    gate = _expand_gates(gate_ref[...], gx_ref[...])
    for j in range(GROUP_SIZE // 2):
        pair = _merge_head_pair(o_c[2 * j * tile:(2 * j + 1) * tile], o_c[(2 * j + 1) * tile:(2 * j + 2) * tile], lane)
        oc_ref[:, j * LANES:(j + 1) * LANES] = (gate[:, j * LANES:(j + 1) * LANES] * pair).astype(BF16)


def _nsa_select(q, gates, kcv, ovl_t, gate_expand, batch, seq):
    tile = SEL_TILE
    n_t = seq // tile
    n_cmp_pad = kcv.shape[0] // batch
    grp = GROUP_SIZE * HEAD_DIM
    q_map = lambda b, h, i: (b * n_t + i, h)
    return pl.pallas_call(
        _nsa_select_kernel,
        grid=(batch, N_KV_HEADS, n_t),
        in_specs=[
            pl.BlockSpec((tile, grp), q_map),
            pl.BlockSpec((tile, LANES), q_map),
            pl.BlockSpec((n_cmp_pad, LANES), lambda b, h, i: (b, 2 * h)),
            pl.BlockSpec((n_cmp_pad, LANES), lambda b, h, i: (b, 2 * h + 1)),
            pl.BlockSpec(ovl_t.shape, lambda b, h, i: (0, 0)),
            pl.BlockSpec(gate_expand.shape, lambda b, h, i: (0, 0)),
        ],
        out_specs=[pl.BlockSpec((tile, GROUP_SIZE * LANES), q_map), pl.BlockSpec((tile, grp), q_map)],
        out_shape=[jax.ShapeDtypeStruct((batch * seq, N_HEADS * LANES), BF16),
                   jax.ShapeDtypeStruct((batch * seq, ATTN_WIDTH), BF16)],
        compiler_params=pltpu.CompilerParams(
            dimension_semantics=("arbitrary", "arbitrary", "arbitrary"), vmem_limit_bytes=VMEM_LIMIT),
        name="nsa_select",
    )(q, gates, kcv, kcv, ovl_t, gate_expand)


def _nsa_attend_kernel(qa_ref, gate_ref, oc_ref, *refs):
    i = pl.program_id(1)
    seq = refs[0].shape[0]
    for blk in range(seq // Q_BLOCK):
        pl.when(i == blk)(functools.partial(_nsa_attend_body, blk, qa_ref, gate_ref, oc_ref, *refs))


def _nsa_attend_body(blk, qa_ref, gate_ref, oc_ref, *refs):
    kv_refs, (diag_ref, edge_ref, gx_ref, o_ref) = refs[:-4], refs[-4:]
    q0 = blk * Q_BLOCK
    q1 = q0 + Q_BLOCK
    w0 = max(q0 - WINDOW, 0)
    lane = lax.broadcasted_iota(jnp.int32, (Q_BLOCK, LANES), 1)
    diag = jnp.concatenate([diag_ref[...]] * 2, axis=0)
    edge = jnp.concatenate([edge_ref[...]] * 2, axis=0)
    gexp = _expand_gates(gate_ref[...], gx_ref[...])
    gates = [gexp[:, :ATTN_WIDTH], gexp[:, ATTN_WIDTH:]]

    def queries(h, j):
        return jnp.concatenate(
            [qa_ref[:, (h * GROUP_SIZE + g) * LANES:(h * GROUP_SIZE + g + 1) * LANES] for g in (2 * j, 2 * j + 1)],
            axis=0)

    def score_mats(branch, h, j):
        ks_ref, _, kw_ref, _ = kv_refs[4 * h:4 * h + 4]
        if branch == "sel":
            s = _dot_nt(queries(h, j), ks_ref[0:q1, :])
            pieces = [s[:, :q0]] if q0 else []
        else:
            s = _dot_nt(queries(h, j), kw_ref[w0:q1, :])
            if q0 >= WINDOW:
                pieces = [s[:, :Q_BLOCK] + edge, s[:, Q_BLOCK:WINDOW]]
            else:
                pieces = [s[:, :q0]] if q0 else []
        return jnp.concatenate(pieces + [s[:, s.shape[1] - Q_BLOCK:] + diag], axis=1)

    def numerators(s):
        return jnp.exp2(s - jnp.max(s, axis=-1, keepdims=True)).astype(BF16)

    def weighted(branch, h, probs):
        _, vs_ref, _, vw_ref = kv_refs[4 * h:4 * h + 4]
        return _dot(probs, vs_ref[0:q1, :] if branch == "sel" else vw_ref[w0:q1, :])

    def normalised(acc, gate):
        even, odd = acc[:Q_BLOCK], acc[Q_BLOCK:]
        num = jnp.where(lane < HEAD_DIM, even, pltpu.roll(odd, HEAD_DIM, 1))
        den = jnp.where(lane < HEAD_DIM, pltpu.roll(even, HEAD_DIM, 1), odd)
        return num * (gate / jnp.maximum(den, 1e-30))

    def merge(h, j, acc_w, acc_s):
        cols = slice((h * (GROUP_SIZE // 2) + j) * LANES, (h * (GROUP_SIZE // 2) + j + 1) * LANES)
        merged = (oc_ref[:, cols].astype(F32) + normalised(acc_s, gates[0][:, cols])
                  + normalised(acc_w, gates[1][:, cols]))
        o_ref[:, cols] = merged.astype(BF16)

    chains = [(branch, h, j) for h in range(N_KV_HEADS) for j in range(GROUP_SIZE // 2)
              for branch in ("win", "sel")]
    scores, probs, outs = {}, {}, {}
    for step in range(len(chains) + 2):
        if step < len(chains):
            scores[step] = score_mats(*chains[step])
        if 0 <= step - 1 < len(chains):
            probs[step - 1] = numerators(scores.pop(step - 1))
        if 0 <= step - 2 < len(chains):
            branch, h, j = chains[step - 2]
            outs[branch] = weighted(branch, h, probs.pop(step - 2))
            if branch == "sel":
                merge(h, j, outs.pop("win"), outs.pop("sel"))


def _nsa_attend(qa, gates, oc, kv8, diag, edge, gate_expand, batch, seq):
    n_qb = seq // Q_BLOCK
    q_map = lambda b, i: (b * n_qb + i, 0)
    fixed = lambda b, i: (0, 0)

    def kv_spec(slab):
        return pl.BlockSpec((seq, LANES), lambda b, i: (b, slab))

    kv_slabs = [branch * 2 * N_KV_HEADS + 2 * h + part
                for h in range(N_KV_HEADS) for branch in range(2) for part in range(2)]
    return pl.pallas_call(
        _nsa_attend_kernel,
        grid=(batch, n_qb),
        in_specs=[
            pl.BlockSpec((Q_BLOCK, N_HEADS * LANES), q_map),
            pl.BlockSpec((Q_BLOCK, N_KV_HEADS * LANES), q_map),
            pl.BlockSpec((Q_BLOCK, ATTN_WIDTH), q_map),
            *[kv_spec(s) for s in kv_slabs],
            pl.BlockSpec(diag.shape, fixed),
            pl.BlockSpec(edge.shape, fixed),
            pl.BlockSpec(gate_expand.shape, fixed),
        ],
        out_specs=pl.BlockSpec((Q_BLOCK, ATTN_WIDTH), q_map),
        out_shape=jax.ShapeDtypeStruct((batch * seq, ATTN_WIDTH), BF16),
        compiler_params=pltpu.CompilerParams(
            dimension_semantics=("arbitrary", "arbitrary"), vmem_limit_bytes=VMEM_LIMIT),
        name="nsa_attend",
    )(qa, gates, oc, *([kv8] * len(kv_slabs)), diag, edge, gate_expand)


def _mix_ffn_ple_kernel(cx_ref, cxh_ref, bg_ref, y_ref, x_ref, p_ref, wc_ref, gc_ref, ga_ref, wo_ref, gf_ref,
                        wu_ref, cu_ref, wd_ref, gp_ref, wpg_ref, wpp_ref, gfin_ref,
                        out_ref, act_ref, n_prev_ref, *, tiles_per_seq):
    step = pl.program_id(0)
    seq_start = step % tiles_per_seq == 0

    @pl.when(step == 0)
    def _():
        n_prev_ref[...] = jnp.zeros_like(n_prev_ref)

    cx = cx_ref[...].astype(F32)
    s1, s2 = _shift_rows(cx, cxh_ref[...].astype(F32), seq_start)
    wc = wc_ref[...]
    y_conv = bg_ref[...].astype(F32) * (wc[0:1] * s2 + wc[1:2] * s1 + wc[2:3] * cx)
    mixed = jnp.concatenate(
        [_rms(y_conv, gc_ref[...]).astype(BF16), _rms(y_ref[...].astype(F32), ga_ref[...]).astype(BF16)], axis=1)
    h = x_ref[...] + _dot(mixed, wo_ref[...])

    n = _rms(h, gf_ref[...]).astype(BF16)
    nh = n_prev_ref[...]
    n_prev_ref[...] = n[n.shape[0] - BF16_ROWS:, :]

    def conv_up(lo):
        w = wu_ref[:, lo:lo + TF_FFN]
        a = _dot(n, w)
        t1, t2 = _shift_rows(a, _dot(nh, w), seq_start)
        c = cu_ref[:, lo:lo + TF_FFN]
        return c[0:1] * t2 + c[1:2] * t1 + c[2:3] * a

    for j in range(D_FF // TF_FFN):
        u_gate = conv_up(j * TF_FFN)
        u_val = conv_up(D_FF + j * TF_FFN)
        act_ref[:, j * TF_FFN:(j + 1) * TF_FFN] = (u_gate * _sigmoid(u_gate) * u_val).astype(BF16)
    h = h + _dot(act_ref[...], wd_ref[...])

    gate = _sigmoid(_dot(_rms(h, gp_ref[...]).astype(BF16), wpg_ref[...]))
    h = h + gate * _dot(p_ref[...].astype(BF16), wpp_ref[...])
    out_ref[...] = _rms(h, gfin_ref[...])


def _mix_ffn_ple(cx, bg, y_attn, x2, p2, w_conv, g_conv, g_attn, w_out, g_ffn, w_up, w_ffn_conv, w_down,
                 g_ple, w_gate, w_proj, g_final, seq):
    t = x2.shape[0]
    tm = TM_FFN
    row = lambda i: (i, 0)
    halo = lambda i: (jnp.maximum(i * (tm // BF16_ROWS) - 1, 0), 0)

    def resident(a):
        return pl.BlockSpec(a.shape, lambda i: (0, 0), pipeline_mode=pl.Buffered(1))

    weights = (w_conv, g_conv, g_attn, w_out, g_ffn, w_up, w_ffn_conv, w_down, g_ple, w_gate, w_proj, g_final)
    return pl.pallas_call(
        functools.partial(_mix_ffn_ple_kernel, tiles_per_seq=seq // tm),
        grid=(t // tm,),
        in_specs=[
            pl.BlockSpec((tm, CONV_WIDTH), row),
            pl.BlockSpec((BF16_ROWS, CONV_WIDTH), halo),
            pl.BlockSpec((tm, CONV_WIDTH), row),
            pl.BlockSpec((tm, ATTN_WIDTH), row),
            pl.BlockSpec((tm, D_MODEL), row),
            pl.BlockSpec((tm, PLE_DIM), row),
            *[resident(w) for w in weights],
        ],
        out_specs=pl.BlockSpec((tm, D_MODEL), row),
        out_shape=jax.ShapeDtypeStruct((t, D_MODEL), F32),
        scratch_shapes=[pltpu.VMEM((tm, D_FF), BF16), pltpu.VMEM((BF16_ROWS, D_MODEL), BF16)],
        compiler_params=pltpu.CompilerParams(
            dimension_semantics=("arbitrary",), vmem_limit_bytes=VMEM_LIMIT),
        name="mix_ffn_ple",
    )(cx, cx, bg, y_attn, x2, p2, *weights)


def _permute_in_proj(w_in):
    per_kv = GROUP_SIZE * N_BRANCH
    assert w_in.shape[1] == COL_GATE + N_KV_HEADS * per_kv
    pad = jnp.zeros((w_in.shape[0], LANES - per_kv), w_in.dtype)
    slabs = [w_in[:, :COL_GATE]]
    for h in range(N_KV_HEADS):
        slabs += [w_in[:, COL_GATE + h * per_kv:COL_GATE + (h + 1) * per_kv], pad]
    return jnp.concatenate(slabs, axis=1).astype(BF16)


def _rope_tables(pos):
    inv = ROPE_THETA ** (-jnp.arange(0, HEAD_DIM, 2, dtype=F32) / HEAD_DIM)
    ang = pos.astype(F32)[:, None] * inv[None, :]
    cos = jnp.concatenate([jnp.cos(ang), jnp.cos(ang)], axis=-1)
    sin = jnp.concatenate([-jnp.sin(ang), jnp.sin(ang)], axis=-1)
    return cos, sin


def _compress_weights(w1_k, w1_v, w2_k, w2_v, pe_k, pe_v):
    n_str = 2 * N_KV_HEADS
    half = CMP_LEN // 2
    w1 = jnp.stack([w1_k, w1_k, w1_v, w1_v]).reshape(n_str, 2, half, HEAD_DIM, HEAD_DIM)
    eye = jnp.eye(n_str, dtype=F32)
    w1_big = jnp.einsum('chrde,cx->rcdhxe', w1, eye).reshape(half * n_str * HEAD_DIM, 2 * n_str * HEAD_DIM)
    pe = jnp.stack([pe_k, pe_k, pe_v, pe_v]).reshape(n_str, 2, half, HEAD_DIM)
    pe_big = jnp.transpose(pe, (1, 2, 0, 3)).reshape(2, half * n_str * HEAD_DIM)
    pe_big = jnp.concatenate([pe_big, jnp.zeros((6, pe_big.shape[1]), F32)], axis=0)
    w2 = jnp.stack([w2_k, w2_k, w2_v, w2_v])
    out_pos = np.asarray([0, 4, 2, 6])
    place = jnp.zeros((n_str, 2 * n_str), F32).at[np.arange(n_str), out_pos].set(1.0)
    w2_big = jnp.einsum('cde,cx->cdxe', w2, place).reshape(n_str * HEAD_DIM, 2 * n_str * HEAD_DIM)
    return w1_big.astype(BF16), pe_big.astype(BF16), w2_big.astype(BF16)


def kernel(x, p, g_mix, w_in, w_conv_mix, cmp_pe_k, cmp_w1_k, cmp_w2_k, cmp_pe_v, cmp_w1_v, cmp_w2_v,
           g_gn_conv, g_gn_attn, w_out, g_ffn, w_up, w_ffn_conv, w_down, g_ple, w_ple_gate, w_ple_proj,
           g_final):
    batch, seq, _ = x.shape
    depth = w_in.shape[0]
    t = batch * seq
    n_cmp_pad = seq // CMP_STRIDE
    n_blk = seq // SEL_BLOCK

    cos, sin = _rope_tables(jnp.arange(seq, dtype=jnp.int32))
    cos_q = jnp.concatenate([cos, cos], axis=-1)
    sin_q = jnp.concatenate([sin, sin], axis=-1)
    cmp_start = jnp.arange(n_cmp_pad) * CMP_STRIDE
    cos_e, sin_e = _rope_tables(cmp_start + CMP_LEN - 1)
    ones = jnp.ones_like(cos_e)
    zeros = jnp.zeros_like(sin_e)
    reps = CMP_ROWS // n_cmp_pad
    cos_c = jnp.tile(jnp.concatenate([cos_e, ones, ones, ones] * N_KV_HEADS, axis=-1), (reps, 1))
    sin_c = jnp.tile(jnp.concatenate([sin_e, zeros, zeros, zeros] * N_KV_HEADS, axis=-1), (reps, 1))
    blk_start = jnp.arange(n_blk) * SEL_BLOCK
    n_cmp = (seq - CMP_LEN) // CMP_STRIDE + 1
    ovl = (jnp.clip(jnp.minimum(cmp_start[:, None] + CMP_LEN, blk_start[None, :] + SEL_BLOCK)
                    - jnp.maximum(cmp_start[:, None], blk_start[None, :]), 0, None).astype(F32) / CMP_LEN)
    ovl = jnp.where(jnp.arange(n_cmp_pad)[:, None] < n_cmp, ovl, 0.0)
    ovl_t = ovl.T.astype(BF16)
    lane = jnp.arange(LANES)[None, :]
    blk_hot = (lane - HEAD_DIM == jnp.arange(seq)[:, None] // SEL_BLOCK).astype(F32)
    qi = jnp.arange(Q_BLOCK)[:, None]
    ki = jnp.arange(Q_BLOCK)[None, :]
    diag = jnp.where(ki <= qi, 0.0, NEG).astype(F32)
    edge = jnp.where(ki > qi, 0.0, NEG).astype(F32)

    g_lane = jnp.arange(LANES)[:, None]
    out_head = jnp.arange(GROUP_SIZE * HEAD_DIM)[None, :] // HEAD_DIM
    gx_cmp = (g_lane == out_head * N_BRANCH).astype(BF16)
    slab_lane = jnp.arange(N_KV_HEADS * LANES)[:, None]
    out_h = jnp.arange(ATTN_WIDTH)[None, :] // HEAD_DIM
    src_lane = (out_h // GROUP_SIZE) * LANES + (out_h % GROUP_SIZE) * N_BRANCH
    gx_att = jnp.concatenate([(slab_lane == src_lane + r).astype(BF16) for r in (1, 2)], axis=1)

    pad_rows = lambda w: jnp.concatenate([w, jnp.zeros((8 - w.shape[0], w.shape[1]), w.dtype)], axis=0)

    h = x.reshape(t, D_MODEL)
    for i in range(depth):
        w_perm = _permute_in_proj(w_in[i])
        cx, bg, q, kvc, kv8, gates = _inproj(h, g_mix[i][None, :], w_perm, cos_q, sin_q, blk_hot, seq)

        w1_big, pe_big, w2_big = _compress_weights(
            cmp_w1_k[i], cmp_w1_v[i], cmp_w2_k[i], cmp_w2_v[i], cmp_pe_k[i], cmp_pe_v[i])
        r = kvc.reshape(t // CMP_STRIDE, CMP_STRIDE * 2 * KV_WIDTH)
        kcv = _compress(r, w1_big, pe_big, w2_big, cos_c, sin_c)

        qa, oc = _nsa_select(q, gates, kcv, ovl_t, gx_cmp, batch, seq)
        y_attn = _nsa_attend(qa, gates, oc, kv8, diag, edge, gx_att, batch, seq)

        assert depth == 1
        h = _mix_ffn_ple(cx, bg, y_attn, h, p[i].reshape(t, PLE_DIM), pad_rows(w_conv_mix[i]),
                         g_gn_conv[i][None, :], g_gn_attn[i][None, :], w_out[i].astype(BF16), g_ffn[i][None, :],
                         w_up[i].astype(BF16), pad_rows(w_ffn_conv[i]), w_down[i].astype(BF16),
                         g_ple[i][None, :], w_ple_gate[i].astype(BF16), w_ple_proj[i].astype(BF16),
                         g_final[None, :], seq)
    return h.reshape(batch, seq, D_MODEL)
```

```python
import functools

import numpy as np
import jax
import jax.numpy as jnp
from jax import lax
from jax.experimental import pallas as pl
from jax.experimental.pallas import tpu as pltpu

F32 = jnp.float32
BF16 = jnp.bfloat16

D_MODEL = 1024
PLE_DIM = 256
CONV_WIDTH = 512
N_HEADS = 8
N_KV_HEADS = 2
HEAD_DIM = 64
HALF_DIM = HEAD_DIM // 2
GROUP_SIZE = N_HEADS // N_KV_HEADS
ATTN_WIDTH = N_HEADS * HEAD_DIM
KV_WIDTH = N_KV_HEADS * HEAD_DIM
N_BRANCH = 3
CMP_LEN = 32
CMP_STRIDE = 16
SEL_BLOCK = 64
N_SEL = 8
WINDOW = 512
Q_BLOCK = 128
D_FF = 2816
ROPE_THETA = 10000.0
EPS = 1e-6
NEG = -1e30
SEL_FORCE = 1e4

LANES = 128
BF16_ROWS = 16
VMEM_LIMIT = 56 * 1024 * 1024

COL_XIN = 0
COL_B = CONV_WIDTH
COL_C = 2 * CONV_WIDTH
COL_Q = 3 * CONV_WIDTH
COL_KVC = COL_Q + ATTN_WIDTH
COL_KV4 = COL_KVC + 2 * KV_WIDTH
COL_GATE = COL_KV4 + 4 * KV_WIDTH
IN_COLS_PAD = COL_GATE + N_KV_HEADS * LANES
KV8_WIDTH = 2 * 2 * N_KV_HEADS * LANES
Q_SCALE = HEAD_DIM ** -0.5 * float(np.log2(np.e))

TM_IN = 1024
TM_FFN = 512
TF_FFN = 256
CMP_ROWS = 512
KEY_STEP = 256
SEL_TILE = 512


def _dot(a, b):
    return jnp.dot(a, b, preferred_element_type=F32)


def _dot_nt(a, b):
    return lax.dot_general(a, b, (((1,), (1,)), ((), ())), preferred_element_type=F32)


def _sigmoid(x):
    return 1.0 / (1.0 + jnp.exp(-x))


def _rms(x, g):
    return x * lax.rsqrt(jnp.mean(x * x, axis=-1, keepdims=True) + EPS) * g


def _rope_slab(y, cos, sin_signed, first_half):
    fwd = pltpu.roll(y, HALF_DIM, 1)
    bwd = pltpu.roll(y, LANES - HALF_DIM, 1)
    return y * cos + jnp.where(first_half, bwd, fwd) * sin_signed


def _shift_rows(a, halo, seq_start):
    halo = jnp.where(seq_start, 0.0, halo)
    h1 = halo[BF16_ROWS - 1:BF16_ROWS, :]
    h2 = halo[BF16_ROWS - 2:BF16_ROWS - 1, :]
    row = lax.broadcasted_iota(jnp.int32, a.shape, 0)
    s1 = jnp.where(row == 0, h1, pltpu.roll(a, 1, 0))
    s2 = jnp.where(row == 0, h2, jnp.where(row == 1, h1, pltpu.roll(a, 2, 0)))
    return s1, s2


def _inproj_kernel(x_ref, g_ref, w_ref, cos_ref, sin_ref, hot_ref,
                   cx_ref, bg_ref, q_ref, kvc_ref, kv8_ref, gate_ref):
    n = _rms(x_ref[...], g_ref[...]).astype(BF16)

    def proj(lo, hi):
        return _dot(n, w_ref[:, lo:hi])

    x_in = proj(COL_XIN, COL_B)
    bg_ref[...] = proj(COL_B, COL_C).astype(BF16)
    cx_ref[...] = (proj(COL_C, COL_Q) * x_in).astype(BF16)

    cos = cos_ref[...]
    sin = sin_ref[...]
    lane = lax.broadcasted_iota(jnp.int32, cos.shape, 1)
    first_half = (lane % HEAD_DIM) < HALF_DIM
    for s in range(ATTN_WIDTH // LANES):
        y = proj(COL_Q + s * LANES, COL_Q + (s + 1) * LANES)
        q_ref[:, s * LANES:(s + 1) * LANES] = (_rope_slab(y, cos, sin, first_half) * Q_SCALE).astype(BF16)

    kvc_ref[...] = proj(COL_KVC, COL_KV4).astype(BF16)

    in_head = lane < HEAD_DIM
    ones_col = jnp.where(in_head, 0.0, 1.0)
    zeros = jnp.zeros_like(cos)
    for branch, key_fill in enumerate((hot_ref[...], zeros)):
        k_pair = _rope_slab(proj(COL_KV4 + (2 * branch) * LANES, COL_KV4 + (2 * branch + 1) * LANES),
                            cos, sin, first_half)
        v_pair = proj(COL_KV4 + (2 * branch + 1) * LANES, COL_KV4 + (2 * branch + 2) * LANES)
        for h in range(N_KV_HEADS):
            k_h = k_pair if h == 0 else pltpu.roll(k_pair, HEAD_DIM, 1)
            v_h = v_pair if h == 0 else pltpu.roll(v_pair, HEAD_DIM, 1)
            slab = (branch * N_KV_HEADS + h) * 2
            kv8_ref[:, slab * LANES:(slab + 1) * LANES] = jnp.where(in_head, k_h, key_fill).astype(BF16)
            kv8_ref[:, (slab + 1) * LANES:(slab + 2) * LANES] = jnp.where(in_head, v_h, ones_col).astype(BF16)

    gate_ref[...] = _sigmoid(proj(COL_GATE, IN_COLS_PAD))


def _inproj(x2, g_mix, w_perm, cos_q, sin_q, blk_hot, seq):
    t = x2.shape[0]
    tm = TM_IN
    n_seq_tiles = seq // tm
    row = lambda i: (i, 0)
    fixed = lambda i: (0, 0)
    tab = lambda i: (i % n_seq_tiles, 0)
    return pl.pallas_call(
        _inproj_kernel,
        grid=(t // tm,),
        in_specs=[
            pl.BlockSpec((tm, D_MODEL), row),
            pl.BlockSpec((1, D_MODEL), fixed),
            pl.BlockSpec((D_MODEL, IN_COLS_PAD), fixed),
            pl.BlockSpec((tm, LANES), tab),
            pl.BlockSpec((tm, LANES), tab),
            pl.BlockSpec((tm, LANES), tab),
        ],
        out_specs=[
            pl.BlockSpec((tm, CONV_WIDTH), row),
            pl.BlockSpec((tm, CONV_WIDTH), row),
            pl.BlockSpec((tm, ATTN_WIDTH), row),
            pl.BlockSpec((tm, 2 * KV_WIDTH), row),
            pl.BlockSpec((tm, KV8_WIDTH), row),
            pl.BlockSpec((tm, N_KV_HEADS * LANES), row),
        ],
        out_shape=[
            jax.ShapeDtypeStruct((t, CONV_WIDTH), BF16),
            jax.ShapeDtypeStruct((t, CONV_WIDTH), BF16),
            jax.ShapeDtypeStruct((t, ATTN_WIDTH), BF16),
            jax.ShapeDtypeStruct((t, 2 * KV_WIDTH), BF16),
            jax.ShapeDtypeStruct((t, KV8_WIDTH), BF16),
            jax.ShapeDtypeStruct((t, N_KV_HEADS * LANES), F32),
        ],
        compiler_params=pltpu.CompilerParams(
            dimension_semantics=("arbitrary",), vmem_limit_bytes=VMEM_LIMIT),
        name="inproj",
    )(x2, g_mix, w_perm, cos_q, sin_q, blk_hot)


def _compress_kernel(r_ref, w1_ref, pe_ref, w2_ref, cos_ref, sin_ref, out_ref):
    width = 2 * KV_WIDTH
    p = _dot(r_ref[...], w1_ref[...])
    pb = _dot(pe_ref[...], w1_ref[...])
    bias = pb[0:1, :width] + pb[1:2, width:]
    rows = p.shape[0]
    nxt = pltpu.roll(p[:, width:], rows - 1, 0)
    pre = p[:, :width] + nxt + bias
    hid = (pre * _sigmoid(pre)).astype(BF16)
    y = _dot(hid, w2_ref[...])
    cos = cos_ref[...]
    sin = sin_ref[...]
    lane = lax.broadcasted_iota(jnp.int32, (rows, LANES), 1)
    first_half = (lane % HEAD_DIM) < HALF_DIM
    for s in range(y.shape[1] // LANES):
        sl = slice(s * LANES, (s + 1) * LANES)
        out_ref[:, sl] = _rope_slab(y[:, sl], cos[:, sl], sin[:, sl], first_half).astype(BF16)


def _compress(r, w1_big, pe_big, w2_big, cos_c, sin_c):
    rows = r.shape[0]
    width = w2_big.shape[1]
    row = lambda i: (i, 0)
    fixed = lambda i: (0, 0)
    return pl.pallas_call(
        _compress_kernel,
        grid=(rows // CMP_ROWS,),
        in_specs=[
            pl.BlockSpec((CMP_ROWS, r.shape[1]), row),
            pl.BlockSpec(w1_big.shape, fixed),
            pl.BlockSpec(pe_big.shape, fixed),
            pl.BlockSpec(w2_big.shape, fixed),
            pl.BlockSpec((CMP_ROWS, width), fixed),
            pl.BlockSpec((CMP_ROWS, width), fixed),
        ],
        out_specs=pl.BlockSpec((CMP_ROWS, width), row),
        out_shape=jax.ShapeDtypeStruct((rows, width), BF16),
        compiler_params=pltpu.CompilerParams(
            dimension_semantics=("arbitrary",), vmem_limit_bytes=VMEM_LIMIT),
        name="compress",
    )(r, w1_big, pe_big, w2_big, cos_c, sin_c)


def _masked_softmax2(s, mask):
    s = jnp.where(mask, s, NEG)
    m = jnp.max(s, axis=-1, keepdims=True)
    e = jnp.where(mask, jnp.exp2(s - m), 0.0)
    return e / jnp.maximum(jnp.sum(e, axis=-1, keepdims=True), 1e-30)


def _stack_heads(q, lane):
    parts = []
    for g in range(GROUP_SIZE):
        slab = q[:, (g // 2) * LANES:(g // 2 + 1) * LANES]
        if g % 2 == 1:
            slab = pltpu.roll(slab, HEAD_DIM, 1)
        parts.append(jnp.where(lane < HEAD_DIM, slab, 0.0))
    return parts


def _merge_head_pair(even, odd, lane):
    return jnp.where(lane < HEAD_DIM, even, pltpu.roll(odd, HEAD_DIM, 1))


def _expand_gates(gate, expand):
    hi = gate.astype(BF16)
    lo = (gate - hi.astype(F32)).astype(BF16)
    return _dot(hi, expand) + _dot(lo, expand)


def _nsa_select_kernel(q_ref, gate_ref, kc_ref, vc_ref, ovl_ref, gx_ref, qa_ref, oc_ref):
    tile = q_ref.shape[0]
    q0 = pl.program_id(2) * tile
    lane = lax.broadcasted_iota(jnp.int32, (tile, LANES), 1)
    parts = _stack_heads(q_ref[...].astype(F32), lane)
    qs = jnp.concatenate(parts, axis=0).astype(BF16)

    s_c = _dot_nt(qs, kc_ref[...])
    t_c = q0 + lax.broadcasted_iota(jnp.int32, s_c.shape, 0) % tile
    end_c = lax.broadcasted_iota(jnp.int32, s_c.shape, 1) * CMP_STRIDE + (CMP_LEN - 1)
    p_c = _masked_softmax2(s_c, end_c <= t_c)
    o_c = _dot(p_c.astype(BF16), vc_ref[...])

    psum = p_c[0:tile]
    for g in range(1, GROUP_SIZE):
        psum = psum + p_c[g * tile:(g + 1) * tile]
    p_hi = psum.astype(BF16)
    p_lo = (psum - p_hi.astype(F32)).astype(BF16)
    ovl = ovl_ref[...]
    imp = _dot_nt(ovl, p_hi) + _dot_nt(ovl, p_lo)
    n_blk = imp.shape[0]
    blk = lax.broadcasted_iota(jnp.int32, imp.shape, 0)
    t_q = q0 + lax.broadcasted_iota(jnp.int32, imp.shape, 1)
    cur = t_q // SEL_BLOCK
    valid = blk * SEL_BLOCK <= t_q
    forced = (blk == 0) | (blk == cur) | (blk == cur - 1)
    score = jnp.where(valid, imp + jnp.where(forced, SEL_FORCE, 0.0), NEG)
    rank = jnp.zeros(imp.shape, F32)
    for m in range(n_blk):
        other = score[m:m + 1, :]
        tie = jnp.where(blk > m, 1.0, 0.0)
        rank = rank + jnp.where(other > score, 1.0, 0.0) + jnp.where(other == score, tie, 0.0)
    sel_neg = jnp.where((rank < float(min(N_SEL, n_blk))) & valid, 0.0, NEG)
    pad_lo = jnp.zeros((HEAD_DIM, tile), F32)
    pad_hi = jnp.zeros((LANES - HEAD_DIM - n_blk, tile), F32)
    sel_rows = jnp.concatenate([pad_lo, sel_neg, pad_hi], axis=0).T
    for g in range(GROUP_SIZE):
        qa_ref[:, g * LANES:(g + 1) * LANES] = (parts[g] + sel_rows).astype(BF16)

    gate = _expand_gates(gate_ref[...], gx_ref[...])
    for j in range(GROUP_SIZE // 2):
        pair = _merge_head_pair(o_c[2 * j * tile:(2 * j + 1) * tile], o_c[(2 * j + 1) * tile:(2 * j + 2) * tile], lane)
        oc_ref[:, j * LANES:(j + 1) * LANES] = (gate[:, j * LANES:(j + 1) * LANES] * pair).astype(BF16)


def _nsa_select(q, gates, kcv, ovl_t, gate_expand, batch, seq):
    tile = SEL_TILE
    n_t = seq // tile
    n_cmp_pad = kcv.shape[0] // batch
    grp = GROUP_SIZE * HEAD_DIM
    q_map = lambda b, h, i: (b * n_t + i, h)
    return pl.pallas_call(
        _nsa_select_kernel,
        grid=(batch, N_KV_HEADS, n_t),
        in_specs=[
            pl.BlockSpec((tile, grp), q_map),
            pl.BlockSpec((tile, LANES), q_map),
            pl.BlockSpec((n_cmp_pad, LANES), lambda b, h, i: (b, 2 * h)),
            pl.BlockSpec((n_cmp_pad, LANES), lambda b, h, i: (b, 2 * h + 1)),
            pl.BlockSpec(ovl_t.shape, lambda b, h, i: (0, 0)),
            pl.BlockSpec(gate_expand.shape, lambda b, h, i: (0, 0)),
        ],
        out_specs=[pl.BlockSpec((tile, GROUP_SIZE * LANES), q_map), pl.BlockSpec((tile, grp), q_map)],
        out_shape=[jax.ShapeDtypeStruct((batch * seq, N_HEADS * LANES), BF16),
                   jax.ShapeDtypeStruct((batch * seq, ATTN_WIDTH), BF16)],
        compiler_params=pltpu.CompilerParams(
            dimension_semantics=("arbitrary", "arbitrary", "arbitrary"), vmem_limit_bytes=VMEM_LIMIT),
        name="nsa_select",
    )(q, gates, kcv, kcv, ovl_t, gate_expand)


def _nsa_attend_kernel(qa_ref, gate_ref, oc_ref, *refs):
    i = pl.program_id(1)
    seq = refs[0].shape[0]
    for pair in range(seq // KEY_STEP):
        pl.when(i // (KEY_STEP // Q_BLOCK) == pair)(functools.partial(
            _nsa_attend_body, pair, qa_ref, gate_ref, oc_ref, *refs))


def _nsa_attend_body(pair, qa_ref, gate_ref, oc_ref, *refs):
    kv_refs, (tri_ref, edge_ref, gx_ref, o_ref) = refs[:-4], refs[-4:]
    i = pl.program_id(1)
    q0 = i * Q_BLOCK
    lane = lax.broadcasted_iota(jnp.int32, (Q_BLOCK, LANES), 1)
    k_end = (pair + 1) * KEY_STEP
    windowed = k_end > Q_BLOCK + WINDOW
    n_win = Q_BLOCK + WINDOW if windowed else k_end
    w0 = pl.multiple_of(q0 - WINDOW, Q_BLOCK) if windowed else 0
    tri = jnp.concatenate([tri_ref[i % (KEY_STEP // Q_BLOCK)]] * 2, axis=0)
    diag = tri_ref[KEY_STEP // Q_BLOCK - 1][:, KEY_STEP - Q_BLOCK:]
    diag = jnp.concatenate([diag] * 2, axis=0)
    edge = jnp.concatenate([edge_ref[...]] * 2, axis=0)
    gexp = _expand_gates(gate_ref[...], gx_ref[...])
    gates = [gexp[:, :ATTN_WIDTH], gexp[:, ATTN_WIDTH:]]

    def causal(s):
        past = s.shape[1] - KEY_STEP
        return jnp.concatenate([s[:, :past], s[:, past:] + tri], axis=1) if past else s + tri

    def queries(h, j):
        return jnp.concatenate(
            [qa_ref[:, (h * GROUP_SIZE + g) * LANES:(h * GROUP_SIZE + g + 1) * LANES] for g in (2 * j, 2 * j + 1)],
            axis=0)

    def score_mats(branch, h, j):
        ks_ref, _, kw_ref, _ = kv_refs[4 * h:4 * h + 4]
        if branch == "sel":
            return causal(_dot_nt(queries(h, j), ks_ref[0:k_end, :]))
        s = _dot_nt(queries(h, j), kw_ref[pl.ds(w0, n_win), :])
        if not windowed:
            return causal(s)
        return jnp.concatenate([s[:, :Q_BLOCK] + edge, s[:, Q_BLOCK:WINDOW], s[:, WINDOW:] + diag], axis=1)

    def numerators(s):
        return jnp.exp2(s - jnp.max(s, axis=-1, keepdims=True)).astype(BF16)

    def weighted(branch, h, probs):
        _, vs_ref, _, vw_ref = kv_refs[4 * h:4 * h + 4]
        return _dot(probs, vs_ref[0:k_end, :] if branch == "sel" else vw_ref[pl.ds(w0, n_win), :])

    def normalised(acc, gate):
        even, odd = acc[:Q_BLOCK], acc[Q_BLOCK:]
        num = jnp.where(lane < HEAD_DIM, even, pltpu.roll(odd, HEAD_DIM, 1))
        den = jnp.where(lane < HEAD_DIM, pltpu.roll(even, HEAD_DIM, 1), odd)
        return num * (gate / jnp.maximum(den, 1e-30))

    def merge(h, j, acc_w, acc_s):
        cols = slice((h * (GROUP_SIZE // 2) + j) * LANES, (h * (GROUP_SIZE // 2) + j + 1) * LANES)
        merged = (oc_ref[:, cols].astype(F32) + normalised(acc_s, gates[0][:, cols])
                  + normalised(acc_w, gates[1][:, cols]))
        o_ref[:, cols] = merged.astype(BF16)

    chains = [(branch, h, j) for h in range(N_KV_HEADS) for j in range(GROUP_SIZE // 2)
              for branch in ("win", "sel")]
    scores, probs, outs = {}, {}, {}
    for step in range(len(chains) + 2):
        if step < len(chains):
            scores[step] = score_mats(*chains[step])
        if 0 <= step - 1 < len(chains):
            probs[step - 1] = numerators(scores.pop(step - 1))
        if 0 <= step - 2 < len(chains):
            branch, h, j = chains[step - 2]
            outs[branch] = weighted(branch, h, probs.pop(step - 2))
            if branch == "sel":
                merge(h, j, outs.pop("win"), outs.pop("sel"))


def _nsa_attend(qa, gates, oc, kv8, tri, edge, gate_expand, batch, seq):
    n_qb = seq // Q_BLOCK
    q_map = lambda b, i: (b * n_qb + i, 0)
    fixed = lambda b, i: (0, 0)
    fixed3 = lambda b, i: (0, 0, 0)

    def kv_spec(slab):
        return pl.BlockSpec((seq, LANES), lambda b, i: (b, slab))

    kv_slabs = [branch * 2 * N_KV_HEADS + 2 * h + part
                for h in range(N_KV_HEADS) for branch in range(2) for part in range(2)]
    return pl.pallas_call(
        _nsa_attend_kernel,
        grid=(batch, n_qb),
        in_specs=[
            pl.BlockSpec((Q_BLOCK, N_HEADS * LANES), q_map),
            pl.BlockSpec((Q_BLOCK, N_KV_HEADS * LANES), q_map),
            pl.BlockSpec((Q_BLOCK, ATTN_WIDTH), q_map),
            *[kv_spec(s) for s in kv_slabs],
            pl.BlockSpec(tri.shape, fixed3),
            pl.BlockSpec(edge.shape, fixed),
            pl.BlockSpec(gate_expand.shape, fixed),
        ],
        out_specs=pl.BlockSpec((Q_BLOCK, ATTN_WIDTH), q_map),
        out_shape=jax.ShapeDtypeStruct((batch * seq, ATTN_WIDTH), BF16),
        compiler_params=pltpu.CompilerParams(
            dimension_semantics=("arbitrary", "arbitrary"), vmem_limit_bytes=VMEM_LIMIT),
        name="nsa_attend",
    )(qa, gates, oc, *([kv8] * len(kv_slabs)), tri, edge, gate_expand)


def _mix_ffn_ple_kernel(cx_ref, cxh_ref, bg_ref, y_ref, x_ref, p_ref, wc_ref, gc_ref, ga_ref, wo_ref, gf_ref,
                        wu_ref, cu_ref, wd_ref, gp_ref, wpg_ref, wpp_ref, gfin_ref,
                        out_ref, act_ref, n_prev_ref, *, tiles_per_seq):
    step = pl.program_id(0)
    seq_start = step % tiles_per_seq == 0

    @pl.when(step == 0)
    def _():
        n_prev_ref[...] = jnp.zeros_like(n_prev_ref)

    cx = cx_ref[...].astype(F32)
    s1, s2 = _shift_rows(cx, cxh_ref[...].astype(F32), seq_start)
    wc = wc_ref[...]
    y_conv = bg_ref[...].astype(F32) * (wc[0:1] * s2 + wc[1:2] * s1 + wc[2:3] * cx)
    mixed = jnp.concatenate(
        [_rms(y_conv, gc_ref[...]).astype(BF16), _rms(y_ref[...].astype(F32), ga_ref[...]).astype(BF16)], axis=1)
    h = x_ref[...] + _dot(mixed, wo_ref[...])

    n = _rms(h, gf_ref[...]).astype(BF16)
    nh = n_prev_ref[...]
    n_prev_ref[...] = n[n.shape[0] - BF16_ROWS:, :]

    def conv_up(lo):
        w = wu_ref[:, lo:lo + TF_FFN]
        a = _dot(n, w)
        t1, t2 = _shift_rows(a, _dot(nh, w), seq_start)
        c = cu_ref[:, lo:lo + TF_FFN]
        return c[0:1] * t2 + c[1:2] * t1 + c[2:3] * a

    for j in range(D_FF // TF_FFN):
        u_gate = conv_up(j * TF_FFN)
        u_val = conv_up(D_FF + j * TF_FFN)
        act_ref[:, j * TF_FFN:(j + 1) * TF_FFN] = (u_gate * _sigmoid(u_gate) * u_val).astype(BF16)
    h = h + _dot(act_ref[...], wd_ref[...])

    gate = _sigmoid(_dot(_rms(h, gp_ref[...]).astype(BF16), wpg_ref[...]))
    h = h + gate * _dot(p_ref[...].astype(BF16), wpp_ref[...])
    out_ref[...] = _rms(h, gfin_ref[...])


def _mix_ffn_ple(cx, bg, y_attn, x2, p2, w_conv, g_conv, g_attn, w_out, g_ffn, w_up, w_ffn_conv, w_down,
                 g_ple, w_gate, w_proj, g_final, seq):
    t = x2.shape[0]
    tm = TM_FFN
    row = lambda i: (i, 0)
    halo = lambda i: (jnp.maximum(i * (tm // BF16_ROWS) - 1, 0), 0)

    def resident(a):
        return pl.BlockSpec(a.shape, lambda i: (0, 0), pipeline_mode=pl.Buffered(1))

    weights = (w_conv, g_conv, g_attn, w_out, g_ffn, w_up, w_ffn_conv, w_down, g_ple, w_gate, w_proj, g_final)
    return pl.pallas_call(
        functools.partial(_mix_ffn_ple_kernel, tiles_per_seq=seq // tm),
        grid=(t // tm,),
        in_specs=[
            pl.BlockSpec((tm, CONV_WIDTH), row),
            pl.BlockSpec((BF16_ROWS, CONV_WIDTH), halo),
            pl.BlockSpec((tm, CONV_WIDTH), row),
            pl.BlockSpec((tm, ATTN_WIDTH), row),
            pl.BlockSpec((tm, D_MODEL), row),
            pl.BlockSpec((tm, PLE_DIM), row),
            *[resident(w) for w in weights],
        ],
        out_specs=pl.BlockSpec((tm, D_MODEL), row),
        out_shape=jax.ShapeDtypeStruct((t, D_MODEL), F32),
        scratch_shapes=[pltpu.VMEM((tm, D_FF), BF16), pltpu.VMEM((BF16_ROWS, D_MODEL), BF16)],
        compiler_params=pltpu.CompilerParams(
            dimension_semantics=("arbitrary",), vmem_limit_bytes=VMEM_LIMIT),
        name="mix_ffn_ple",
    )(cx, cx, bg, y_attn, x2, p2, *weights)


def _permute_in_proj(w_in):
    per_kv = GROUP_SIZE * N_BRANCH
    assert w_in.shape[1] == COL_GATE + N_KV_HEADS * per_kv
    pad = jnp.zeros((w_in.shape[0], LANES - per_kv), w_in.dtype)
    slabs = [w_in[:, :COL_GATE]]
    for h in range(N_KV_HEADS):
        slabs += [w_in[:, COL_GATE + h * per_kv:COL_GATE + (h + 1) * per_kv], pad]
    return jnp.concatenate(slabs, axis=1).astype(BF16)


def _rope_tables(pos):
    inv = ROPE_THETA ** (-jnp.arange(0, HEAD_DIM, 2, dtype=F32) / HEAD_DIM)
    ang = pos.astype(F32)[:, None] * inv[None, :]
    cos = jnp.concatenate([jnp.cos(ang), jnp.cos(ang)], axis=-1)
    sin = jnp.concatenate([-jnp.sin(ang), jnp.sin(ang)], axis=-1)
    return cos, sin


def _compress_weights(w1_k, w1_v, w2_k, w2_v, pe_k, pe_v):
    n_str = 2 * N_KV_HEADS
    half = CMP_LEN // 2
    w1 = jnp.stack([w1_k, w1_k, w1_v, w1_v]).reshape(n_str, 2, half, HEAD_DIM, HEAD_DIM)
    eye = jnp.eye(n_str, dtype=F32)
    w1_big = jnp.einsum('chrde,cx->rcdhxe', w1, eye).reshape(half * n_str * HEAD_DIM, 2 * n_str * HEAD_DIM)
    pe = jnp.stack([pe_k, pe_k, pe_v, pe_v]).reshape(n_str, 2, half, HEAD_DIM)
    pe_big = jnp.transpose(pe, (1, 2, 0, 3)).reshape(2, half * n_str * HEAD_DIM)
    pe_big = jnp.concatenate([pe_big, jnp.zeros((6, pe_big.shape[1]), F32)], axis=0)
    w2 = jnp.stack([w2_k, w2_k, w2_v, w2_v])
    out_pos = np.asarray([0, 4, 2, 6])
    place = jnp.zeros((n_str, 2 * n_str), F32).at[np.arange(n_str), out_pos].set(1.0)
    w2_big = jnp.einsum('cde,cx->cdxe', w2, place).reshape(n_str * HEAD_DIM, 2 * n_str * HEAD_DIM)
    return w1_big.astype(BF16), pe_big.astype(BF16), w2_big.astype(BF16)


def kernel(x, p, g_mix, w_in, w_conv_mix, cmp_pe_k, cmp_w1_k, cmp_w2_k, cmp_pe_v, cmp_w1_v, cmp_w2_v,
           g_gn_conv, g_gn_attn, w_out, g_ffn, w_up, w_ffn_conv, w_down, g_ple, w_ple_gate, w_ple_proj,
           g_final):
    batch, seq, _ = x.shape
    depth = w_in.shape[0]
    t = batch * seq
    n_cmp_pad = seq // CMP_STRIDE
    n_blk = seq // SEL_BLOCK

    cos, sin = _rope_tables(jnp.arange(seq, dtype=jnp.int32))
    cos_q = jnp.concatenate([cos, cos], axis=-1)
    sin_q = jnp.concatenate([sin, sin], axis=-1)
    cmp_start = jnp.arange(n_cmp_pad) * CMP_STRIDE
    cos_e, sin_e = _rope_tables(cmp_start + CMP_LEN - 1)
    ones = jnp.ones_like(cos_e)
    zeros = jnp.zeros_like(sin_e)
    reps = CMP_ROWS // n_cmp_pad
    cos_c = jnp.tile(jnp.concatenate([cos_e, ones, ones, ones] * N_KV_HEADS, axis=-1), (reps, 1))
    sin_c = jnp.tile(jnp.concatenate([sin_e, zeros, zeros, zeros] * N_KV_HEADS, axis=-1), (reps, 1))
    blk_start = jnp.arange(n_blk) * SEL_BLOCK
    n_cmp = (seq - CMP_LEN) // CMP_STRIDE + 1
    ovl = (jnp.clip(jnp.minimum(cmp_start[:, None] + CMP_LEN, blk_start[None, :] + SEL_BLOCK)
                    - jnp.maximum(cmp_start[:, None], blk_start[None, :]), 0, None).astype(F32) / CMP_LEN)
    ovl = jnp.where(jnp.arange(n_cmp_pad)[:, None] < n_cmp, ovl, 0.0)
    ovl_t = ovl.T.astype(BF16)
    lane = jnp.arange(LANES)[None, :]
    blk_hot = (lane - HEAD_DIM == jnp.arange(seq)[:, None] // SEL_BLOCK).astype(F32)
    qi = jnp.arange(Q_BLOCK)[None, :, None]
    n_pos = KEY_STEP // Q_BLOCK
    tri = jnp.where(jnp.arange(KEY_STEP)[None, None, :] <= jnp.arange(n_pos)[:, None, None] * Q_BLOCK + qi,
                    0.0, NEG).astype(F32)
    edge = jnp.where(jnp.arange(Q_BLOCK)[None, :] > jnp.arange(Q_BLOCK)[:, None], 0.0, NEG).astype(F32)

    g_lane = jnp.arange(LANES)[:, None]
    out_head = jnp.arange(GROUP_SIZE * HEAD_DIM)[None, :] // HEAD_DIM
    gx_cmp = (g_lane == out_head * N_BRANCH).astype(BF16)
    slab_lane = jnp.arange(N_KV_HEADS * LANES)[:, None]
    out_h = jnp.arange(ATTN_WIDTH)[None, :] // HEAD_DIM
    src_lane = (out_h // GROUP_SIZE) * LANES + (out_h % GROUP_SIZE) * N_BRANCH
    gx_att = jnp.concatenate([(slab_lane == src_lane + r).astype(BF16) for r in (1, 2)], axis=1)

    pad_rows = lambda w: jnp.concatenate([w, jnp.zeros((8 - w.shape[0], w.shape[1]), w.dtype)], axis=0)

    h = x.reshape(t, D_MODEL)
    for i in range(depth):
        w_perm = _permute_in_proj(w_in[i])
        cx, bg, q, kvc, kv8, gates = _inproj(h, g_mix[i][None, :], w_perm, cos_q, sin_q, blk_hot, seq)

        w1_big, pe_big, w2_big = _compress_weights(
            cmp_w1_k[i], cmp_w1_v[i], cmp_w2_k[i], cmp_w2_v[i], cmp_pe_k[i], cmp_pe_v[i])
        r = kvc.reshape(t // CMP_STRIDE, CMP_STRIDE * 2 * KV_WIDTH)
        kcv = _compress(r, w1_big, pe_big, w2_big, cos_c, sin_c)

        qa, oc = _nsa_select(q, gates, kcv, ovl_t, gx_cmp, batch, seq)
        y_attn = _nsa_attend(qa, gates, oc, kv8, tri, edge, gx_att, batch, seq)

        assert depth == 1
        h = _mix_ffn_ple(cx, bg, y_attn, h, p[i].reshape(t, PLE_DIM), pad_rows(w_conv_mix[i]),
                         g_gn_conv[i][None, :], g_gn_attn[i][None, :], w_out[i].astype(BF16), g_ffn[i][None, :],
                         w_up[i].astype(BF16), pad_rows(w_ffn_conv[i]), w_down[i].astype(BF16),
                         g_ple[i][None, :], w_ple_gate[i].astype(BF16), w_ple_proj[i].astype(BF16),
                         g_final[None, :], seq)
    return h.reshape(batch, seq, D_MODEL)
```

```python
import functools

import numpy as np
import jax
import jax.numpy as jnp
from jax import lax
from jax.experimental import pallas as pl
from jax.experimental.pallas import tpu as pltpu

F32 = jnp.float32
BF16 = jnp.bfloat16

D_MODEL = 1024
PLE_DIM = 256
CONV_WIDTH = 512
N_HEADS = 8
N_KV_HEADS = 2
HEAD_DIM = 64
HALF_DIM = HEAD_DIM // 2
GROUP_SIZE = N_HEADS // N_KV_HEADS
ATTN_WIDTH = N_HEADS * HEAD_DIM
KV_WIDTH = N_KV_HEADS * HEAD_DIM
N_BRANCH = 3
CMP_LEN = 32
CMP_STRIDE = 16
SEL_BLOCK = 64
N_SEL = 8
WINDOW = 512
Q_BLOCK = 128
D_FF = 2816
ROPE_THETA = 10000.0
EPS = 1e-6
NEG = -1e30
SEL_FORCE = 1e4

LANES = 128
MXU_COLS = 256
BF16_ROWS = 16
VMEM_LIMIT = 56 * 1024 * 1024

COL_XIN = 0
COL_B = CONV_WIDTH
COL_C = 2 * CONV_WIDTH
COL_Q = 3 * CONV_WIDTH
COL_KVC = COL_Q + ATTN_WIDTH
COL_KV4 = COL_KVC + 2 * KV_WIDTH
COL_GATE = COL_KV4 + 4 * KV_WIDTH
IN_COLS_PAD = COL_GATE + N_KV_HEADS * LANES
KV8_WIDTH = 2 * 2 * N_KV_HEADS * LANES
Q_SCALE = HEAD_DIM ** -0.5 * float(np.log2(np.e))

TM_IN = 1024
TM_FFN = 512
TF_FFN = 256
CMP_ROWS = 512
KEY_STEP = 256
SEL_TILE = 512


def _dot(a, b):
    return jnp.dot(a, b, preferred_element_type=F32)


def _dot_nt(a, b):
    return lax.dot_general(a, b, (((1,), (1,)), ((), ())), preferred_element_type=F32)


def _sigmoid(x):
    return 1.0 / (1.0 + jnp.exp(-x))


def _rms(x, g):
    return x * lax.rsqrt(jnp.mean(x * x, axis=-1, keepdims=True) + EPS) * g


def _rope_slab(y, cos, sin_signed, first_half):
    fwd = pltpu.roll(y, HALF_DIM, 1)
    bwd = pltpu.roll(y, LANES - HALF_DIM, 1)
    return y * cos + jnp.where(first_half, bwd, fwd) * sin_signed


def _shift_rows(a, halo, seq_start):
    halo = jnp.where(seq_start, 0.0, halo)
    h1 = halo[BF16_ROWS - 1:BF16_ROWS, :]
    h2 = halo[BF16_ROWS - 2:BF16_ROWS - 1, :]
    row = lax.broadcasted_iota(jnp.int32, a.shape, 0)
    s1 = jnp.where(row == 0, h1, pltpu.roll(a, 1, 0))
    s2 = jnp.where(row == 0, h2, jnp.where(row == 1, h1, pltpu.roll(a, 2, 0)))
    return s1, s2


def _inproj_kernel(x_ref, g_ref, w_ref, cos_ref, sin_ref, hot_ref,
                   cx_ref, bg_ref, q_ref, kvc_ref, kv8_ref, gate_ref):
    n = _rms(x_ref[...], g_ref[...]).astype(BF16)

    def proj(lo, hi):
        return _dot(n, w_ref[:, lo:hi])

    x_in = proj(COL_XIN, COL_B)
    bg_ref[...] = proj(COL_B, COL_C).astype(BF16)
    cx_ref[...] = (proj(COL_C, COL_Q) * x_in).astype(BF16)

    cos = cos_ref[...]
    sin = sin_ref[...]
    lane = lax.broadcasted_iota(jnp.int32, cos.shape, 1)
    first_half = (lane % HEAD_DIM) < HALF_DIM
    for s in range(0, ATTN_WIDTH, MXU_COLS):
        y = proj(COL_Q + s, COL_Q + s + MXU_COLS)
        for lo in range(0, MXU_COLS, LANES):
            q_ref[:, s + lo:s + lo + LANES] = (
                _rope_slab(y[:, lo:lo + LANES], cos, sin, first_half) * Q_SCALE).astype(BF16)

    kvc_ref[...] = proj(COL_KVC, COL_KV4).astype(BF16)

    in_head = lane < HEAD_DIM
    ones_col = jnp.where(in_head, 0.0, 1.0)
    zeros = jnp.zeros_like(cos)
    for branch, key_fill in enumerate((hot_ref[...], zeros)):
        kv_pair = proj(COL_KV4 + branch * MXU_COLS, COL_KV4 + (branch + 1) * MXU_COLS)
        k_pair = _rope_slab(kv_pair[:, :LANES], cos, sin, first_half)
        v_pair = kv_pair[:, LANES:]
        for h in range(N_KV_HEADS):
            k_h = k_pair if h == 0 else pltpu.roll(k_pair, HEAD_DIM, 1)
            v_h = v_pair if h == 0 else pltpu.roll(v_pair, HEAD_DIM, 1)
            slab = (branch * N_KV_HEADS + h) * 2
            kv8_ref[:, slab * LANES:(slab + 1) * LANES] = jnp.where(in_head, k_h, key_fill).astype(BF16)
            kv8_ref[:, (slab + 1) * LANES:(slab + 2) * LANES] = jnp.where(in_head, v_h, ones_col).astype(BF16)

    gate_ref[...] = _sigmoid(proj(COL_GATE, IN_COLS_PAD))


def _inproj(x2, g_mix, w_perm, cos_q, sin_q, blk_hot, seq):
    t = x2.shape[0]
    tm = TM_IN
    n_seq_tiles = seq // tm
    row = lambda i: (i, 0)
    fixed = lambda i: (0, 0)
    tab = lambda i: (i % n_seq_tiles, 0)
    return pl.pallas_call(
        _inproj_kernel,
        grid=(t // tm,),
        in_specs=[
            pl.BlockSpec((tm, D_MODEL), row),
            pl.BlockSpec((1, D_MODEL), fixed),
            pl.BlockSpec((D_MODEL, IN_COLS_PAD), fixed),
            pl.BlockSpec((tm, LANES), tab),
            pl.BlockSpec((tm, LANES), tab),
            pl.BlockSpec((tm, LANES), tab),
        ],
        out_specs=[
            pl.BlockSpec((tm, CONV_WIDTH), row),
            pl.BlockSpec((tm, CONV_WIDTH), row),
            pl.BlockSpec((tm, ATTN_WIDTH), row),
            pl.BlockSpec((tm, 2 * KV_WIDTH), row),
            pl.BlockSpec((tm, KV8_WIDTH), row),
            pl.BlockSpec((tm, N_KV_HEADS * LANES), row),
        ],
        out_shape=[
            jax.ShapeDtypeStruct((t, CONV_WIDTH), BF16),
            jax.ShapeDtypeStruct((t, CONV_WIDTH), BF16),
            jax.ShapeDtypeStruct((t, ATTN_WIDTH), BF16),
            jax.ShapeDtypeStruct((t, 2 * KV_WIDTH), BF16),
            jax.ShapeDtypeStruct((t, KV8_WIDTH), BF16),
            jax.ShapeDtypeStruct((t, N_KV_HEADS * LANES), F32),
        ],
        compiler_params=pltpu.CompilerParams(
            dimension_semantics=("arbitrary",), vmem_limit_bytes=VMEM_LIMIT),
        name="inproj",
    )(x2, g_mix, w_perm, cos_q, sin_q, blk_hot)


def _compress_kernel(r_ref, w1_ref, pe_ref, w2_ref, cos_ref, sin_ref, out_ref):
    width = 2 * KV_WIDTH
    p = _dot(r_ref[...], w1_ref[...])
    pb = _dot(pe_ref[...], w1_ref[...])
    bias = pb[0:1, :width] + pb[1:2, width:]
    rows = p.shape[0]
    nxt = pltpu.roll(p[:, width:], rows - 1, 0)
    pre = p[:, :width] + nxt + bias
    hid = (pre * _sigmoid(pre)).astype(BF16)
    y = _dot(hid, w2_ref[...])
    cos = cos_ref[...]
    sin = sin_ref[...]
    lane = lax.broadcasted_iota(jnp.int32, (rows, LANES), 1)
    first_half = (lane % HEAD_DIM) < HALF_DIM
    for s in range(y.shape[1] // LANES):
        sl = slice(s * LANES, (s + 1) * LANES)
        out_ref[:, sl] = _rope_slab(y[:, sl], cos[:, sl], sin[:, sl], first_half).astype(BF16)


def _compress(r, w1_big, pe_big, w2_big, cos_c, sin_c):
    rows = r.shape[0]
    width = w2_big.shape[1]
    row = lambda i: (i, 0)
    fixed = lambda i: (0, 0)
    return pl.pallas_call(
        _compress_kernel,
        grid=(rows // CMP_ROWS,),
        in_specs=[
            pl.BlockSpec((CMP_ROWS, r.shape[1]), row),
            pl.BlockSpec(w1_big.shape, fixed),
            pl.BlockSpec(pe_big.shape, fixed),
            pl.BlockSpec(w2_big.shape, fixed),
            pl.BlockSpec((CMP_ROWS, width), fixed),
            pl.BlockSpec((CMP_ROWS, width), fixed),
        ],
        out_specs=pl.BlockSpec((CMP_ROWS, width), row),
        out_shape=jax.ShapeDtypeStruct((rows, width), BF16),
        compiler_params=pltpu.CompilerParams(
            dimension_semantics=("arbitrary",), vmem_limit_bytes=VMEM_LIMIT),
        name="compress",
    )(r, w1_big, pe_big, w2_big, cos_c, sin_c)


def _stack_heads(q, lane):
    parts = []
    for g in range(GROUP_SIZE):
        slab = q[:, (g // 2) * LANES:(g // 2 + 1) * LANES]
        if g % 2 == 1:
            slab = pltpu.roll(slab, HEAD_DIM, 1)
        parts.append(jnp.where(lane < HEAD_DIM, slab, 0.0))
    return parts


def _merge_head_pair(even, odd, lane):
    return jnp.where(lane < HEAD_DIM, even, pltpu.roll(odd, HEAD_DIM, 1))


def _expand_gates(gate, expand):
    hi = gate.astype(BF16)
    lo = (gate - hi.astype(F32)).astype(BF16)
    return _dot(hi, expand) + _dot(lo, expand)


def _nsa_select_kernel(q_ref, gate_ref, kc_ref, vc_ref, ovl_ref, gx_ref, qa_ref, oc_ref):
    tile = q_ref.shape[0]
    q0 = pl.program_id(2) * tile
    lane = lax.broadcasted_iota(jnp.int32, (tile, LANES), 1)
    parts = _stack_heads(q_ref[...].astype(F32), lane)
    qs = jnp.concatenate(parts, axis=0).astype(BF16)

    n_cmp = kc_ref.shape[0]
    t_row = q0 + lax.broadcasted_iota(jnp.int32, (tile, n_cmp), 0)
    end_c = lax.broadcasted_iota(jnp.int32, (tile, n_cmp), 1) * CMP_STRIDE + (CMP_LEN - 1)
    bias = jnp.where(end_c <= t_row, 0.0, NEG)
    s_c = _dot_nt(qs, kc_ref[...]) + jnp.concatenate([bias] * GROUP_SIZE, axis=0)
    e_c = jnp.exp2(s_c - jnp.max(s_c, axis=-1, keepdims=True))
    p_c = e_c / jnp.maximum(jnp.sum(e_c, axis=-1, keepdims=True), 1e-30)
    live = t_row[:CMP_LEN, :] >= CMP_LEN - 1
    p_c = jnp.concatenate(
        [piece for g in range(GROUP_SIZE)
         for piece in (jnp.where(live, p_c[g * tile:g * tile + CMP_LEN], 0.0),
                       p_c[g * tile + CMP_LEN:(g + 1) * tile])],
        axis=0)
    o_c = _dot(p_c.astype(BF16), vc_ref[...])

    psum = p_c[0:tile]
    for g in range(1, GROUP_SIZE):
        psum = psum + p_c[g * tile:(g + 1) * tile]
    p_hi = psum.astype(BF16)
    p_lo = (psum - p_hi.astype(F32)).astype(BF16)
    ovl = ovl_ref[...]
    imp = _dot_nt(ovl, p_hi) + _dot_nt(ovl, p_lo)
    n_blk = imp.shape[0]
    blk = lax.broadcasted_iota(jnp.int32, imp.shape, 0)
    t_q = q0 + lax.broadcasted_iota(jnp.int32, imp.shape, 1)
    cur = t_q // SEL_BLOCK
    valid = blk * SEL_BLOCK <= t_q
    forced = (blk == 0) | (blk == cur) | (blk == cur - 1)
    score = jnp.where(valid, imp + jnp.where(forced, SEL_FORCE, 0.0), NEG)
    sub = 8
    groups = [score[r:r + sub] for r in range(0, n_blk, sub)]
    ranks = [jnp.zeros_like(g) for g in groups]
    row = lax.broadcasted_iota(jnp.int32, groups[0].shape, 0)
    for m in range(n_blk):
        other = score[m:m + 1, :]
        for r, g in enumerate(groups):
            if r < m // sub:
                ranks[r] = ranks[r] + jnp.where(other > g, 1.0, 0.0)
            elif r > m // sub:
                ranks[r] = ranks[r] + jnp.where(other >= g, 1.0, 0.0)
            else:
                tie = jnp.where(row > m % sub, 1.0, 0.0)
                ranks[r] = ranks[r] + jnp.where(other > g, 1.0, 0.0) + jnp.where(other == g, tie, 0.0)
    rank = jnp.concatenate(ranks, axis=0)
    sel_neg = jnp.where((rank < float(min(N_SEL, n_blk))) & valid, 0.0, NEG)
    pad_lo = jnp.zeros((HEAD_DIM, tile), F32)
    pad_hi = jnp.zeros((LANES - HEAD_DIM - n_blk, tile), F32)
    sel_rows = jnp.concatenate([pad_lo, sel_neg, pad_hi], axis=0).T
    for g in range(GROUP_SIZE):
        qa_ref[:, g * LANES:(g + 1) * LANES] = (parts[g] + sel_rows).astype(BF16)

    gate = _expand_gates(gate_ref[...], gx_ref[...])
    for j in range(GROUP_SIZE // 2):
        pair = _merge_head_pair(o_c[2 * j * tile:(2 * j + 1) * tile], o_c[(2 * j + 1) * tile:(2 * j + 2) * tile], lane)
        oc_ref[:, j * LANES:(j + 1) * LANES] = (gate[:, j * LANES:(j + 1) * LANES] * pair).astype(BF16)


def _nsa_select(q, gates, kcv, ovl_t, gate_expand, batch, seq):
    tile = SEL_TILE
    n_t = seq // tile
    n_cmp_pad = kcv.shape[0] // batch
    grp = GROUP_SIZE * HEAD_DIM
    q_map = lambda b, h, i: (b * n_t + i, h)
    return pl.pallas_call(
        _nsa_select_kernel,
        grid=(batch, N_KV_HEADS, n_t),
        in_specs=[
            pl.BlockSpec((tile, grp), q_map),
            pl.BlockSpec((tile, LANES), q_map),
            pl.BlockSpec((n_cmp_pad, LANES), lambda b, h, i: (b, 2 * h)),
            pl.BlockSpec((n_cmp_pad, LANES), lambda b, h, i: (b, 2 * h + 1)),
            pl.BlockSpec(ovl_t.shape, lambda b, h, i: (0, 0)),
            pl.BlockSpec(gate_expand.shape, lambda b, h, i: (0, 0)),
        ],
        out_specs=[pl.BlockSpec((tile, GROUP_SIZE * LANES), q_map), pl.BlockSpec((tile, grp), q_map)],
        out_shape=[jax.ShapeDtypeStruct((batch * seq, N_HEADS * LANES), BF16),
                   jax.ShapeDtypeStruct((batch * seq, ATTN_WIDTH), BF16)],
        compiler_params=pltpu.CompilerParams(
            dimension_semantics=("arbitrary", "arbitrary", "arbitrary"), vmem_limit_bytes=VMEM_LIMIT),
        name="nsa_select",
    )(q, gates, kcv, kcv, ovl_t, gate_expand)


def _nsa_attend_kernel(qa_ref, gate_ref, oc_ref, *refs):
    i = pl.program_id(1)
    seq = refs[0].shape[0]
    for pair in range(seq // KEY_STEP):
        pl.when(i // (KEY_STEP // Q_BLOCK) == pair)(functools.partial(
            _nsa_attend_body, pair, qa_ref, gate_ref, oc_ref, *refs))


def _nsa_attend_body(pair, qa_ref, gate_ref, oc_ref, *refs):
    kv_refs, (tri_ref, edge_ref, gx_ref, o_ref) = refs[:-4], refs[-4:]
    i = pl.program_id(1)
    q0 = i * Q_BLOCK
    lane = lax.broadcasted_iota(jnp.int32, (Q_BLOCK, LANES), 1)
    k_end = (pair + 1) * KEY_STEP
    windowed = k_end > Q_BLOCK + WINDOW
    n_win = Q_BLOCK + WINDOW if windowed else k_end
    w0 = pl.multiple_of(q0 - WINDOW, Q_BLOCK) if windowed else 0
    tri = jnp.concatenate([tri_ref[i % (KEY_STEP // Q_BLOCK)]] * 2, axis=0)
    diag = tri_ref[KEY_STEP // Q_BLOCK - 1][:, KEY_STEP - Q_BLOCK:]
    diag = jnp.concatenate([diag] * 2, axis=0)
    edge = jnp.concatenate([edge_ref[...]] * 2, axis=0)
    gexp = _expand_gates(gate_ref[...], gx_ref[...])
    gates = [gexp[:, :ATTN_WIDTH], gexp[:, ATTN_WIDTH:]]

    def causal(s):
        past = s.shape[1] - KEY_STEP
        return jnp.concatenate([s[:, :past], s[:, past:] + tri], axis=1) if past else s + tri

    def queries(h, j):
        return jnp.concatenate(
            [qa_ref[:, (h * GROUP_SIZE + g) * LANES:(h * GROUP_SIZE + g + 1) * LANES] for g in (2 * j, 2 * j + 1)],
            axis=0)

    def score_mats(branch, h, j):
        ks_ref, _, kw_ref, _ = kv_refs[4 * h:4 * h + 4]
        if branch == "sel":
            return causal(_dot_nt(queries(h, j), ks_ref[0:k_end, :]))
        s = _dot_nt(queries(h, j), kw_ref[pl.ds(w0, n_win), :])
        if not windowed:
            return causal(s)
        return jnp.concatenate([s[:, :Q_BLOCK] + edge, s[:, Q_BLOCK:WINDOW], s[:, WINDOW:] + diag], axis=1)

    def numerators(s):
        return jnp.exp2(s - jnp.max(s, axis=-1, keepdims=True)).astype(BF16)

    def weighted(branch, h, probs):
        _, vs_ref, _, vw_ref = kv_refs[4 * h:4 * h + 4]
        return _dot(probs, vs_ref[0:k_end, :] if branch == "sel" else vw_ref[pl.ds(w0, n_win), :])

    def normalised(acc, gate):
        even, odd = acc[:Q_BLOCK], acc[Q_BLOCK:]
        num = jnp.where(lane < HEAD_DIM, even, pltpu.roll(odd, HEAD_DIM, 1))
        den = jnp.where(lane < HEAD_DIM, pltpu.roll(even, HEAD_DIM, 1), odd)
        return num * (gate / jnp.maximum(den, 1e-30))

    def merge(h, j, acc_w, acc_s):
        cols = slice((h * (GROUP_SIZE // 2) + j) * LANES, (h * (GROUP_SIZE // 2) + j + 1) * LANES)
        merged = (oc_ref[:, cols].astype(F32) + normalised(acc_s, gates[0][:, cols])
                  + normalised(acc_w, gates[1][:, cols]))
        o_ref[:, cols] = merged.astype(BF16)

    chains = [(branch, h, j) for h in range(N_KV_HEADS) for j in range(GROUP_SIZE // 2)
              for branch in ("win", "sel")]
    scores, probs, outs = {}, {}, {}
    for step in range(len(chains) + 2):
        if step < len(chains):
            scores[step] = score_mats(*chains[step])
        if 0 <= step - 1 < len(chains):
            probs[step - 1] = numerators(scores.pop(step - 1))
        if 0 <= step - 2 < len(chains):
            branch, h, j = chains[step - 2]
            outs[branch] = weighted(branch, h, probs.pop(step - 2))
            if branch == "sel":
                merge(h, j, outs.pop("win"), outs.pop("sel"))


def _nsa_attend(qa, gates, oc, kv8, tri, edge, gate_expand, batch, seq):
    n_qb = seq // Q_BLOCK
    q_map = lambda b, i: (b * n_qb + i, 0)
    fixed = lambda b, i: (0, 0)
    fixed3 = lambda b, i: (0, 0, 0)

    def kv_spec(slab):
        return pl.BlockSpec((seq, LANES), lambda b, i: (b, slab))

    kv_slabs = [branch * 2 * N_KV_HEADS + 2 * h + part
                for h in range(N_KV_HEADS) for branch in range(2) for part in range(2)]
    return pl.pallas_call(
        _nsa_attend_kernel,
        grid=(batch, n_qb),
        in_specs=[
            pl.BlockSpec((Q_BLOCK, N_HEADS * LANES), q_map),
            pl.BlockSpec((Q_BLOCK, N_KV_HEADS * LANES), q_map),
            pl.BlockSpec((Q_BLOCK, ATTN_WIDTH), q_map),
            *[kv_spec(s) for s in kv_slabs],
            pl.BlockSpec(tri.shape, fixed3),
            pl.BlockSpec(edge.shape, fixed),
            pl.BlockSpec(gate_expand.shape, fixed),
        ],
        out_specs=pl.BlockSpec((Q_BLOCK, ATTN_WIDTH), q_map),
        out_shape=jax.ShapeDtypeStruct((batch * seq, ATTN_WIDTH), BF16),
        compiler_params=pltpu.CompilerParams(
            dimension_semantics=("arbitrary", "arbitrary"), vmem_limit_bytes=VMEM_LIMIT),
        name="nsa_attend",
    )(qa, gates, oc, *([kv8] * len(kv_slabs)), tri, edge, gate_expand)


def _mix_ffn_ple_kernel(cx_ref, cxh_ref, bg_ref, y_ref, x_ref, p_ref, wc_ref, gc_ref, ga_ref, wo_ref, gf_ref,
                        wu_ref, cu_ref, wd_ref, gp_ref, wpg_ref, wpp_ref, gfin_ref,
                        out_ref, act_ref, n_prev_ref, *, tiles_per_seq):
    step = pl.program_id(0)
    seq_start = step % tiles_per_seq == 0

    @pl.when(step == 0)
    def _():
        n_prev_ref[...] = jnp.zeros_like(n_prev_ref)

    cx = cx_ref[...].astype(F32)
    s1, s2 = _shift_rows(cx, cxh_ref[...].astype(F32), seq_start)
    wc = wc_ref[...]
    y_conv = bg_ref[...].astype(F32) * (wc[0:1] * s2 + wc[1:2] * s1 + wc[2:3] * cx)
    mixed = jnp.concatenate(
        [_rms(y_conv, gc_ref[...]).astype(BF16), _rms(y_ref[...].astype(F32), ga_ref[...]).astype(BF16)], axis=1)
    h = x_ref[...] + _dot(mixed, wo_ref[...])

    n = _rms(h, gf_ref[...]).astype(BF16)
    nh = n_prev_ref[...]
    n_prev_ref[...] = n[n.shape[0] - BF16_ROWS:, :]

    n_ext = jnp.concatenate([nh, n], axis=0)

    def conv_up(lo):
        a_ext = _dot(n_ext, wu_ref[:, lo:lo + TF_FFN])
        a = a_ext[BF16_ROWS:]
        t1, t2 = _shift_rows(a, a_ext[:BF16_ROWS], seq_start)
        c = cu_ref[:, lo:lo + TF_FFN]
        return c[0:1] * t2 + c[1:2] * t1 + c[2:3] * a

    for j in range(D_FF // TF_FFN):
        u_gate = conv_up(j * TF_FFN)
        u_val = conv_up(D_FF + j * TF_FFN)
        act_ref[:, j * TF_FFN:(j + 1) * TF_FFN] = (u_gate * _sigmoid(u_gate) * u_val).astype(BF16)
    h = h + _dot(act_ref[...], wd_ref[...])

    gate = _sigmoid(_dot(_rms(h, gp_ref[...]).astype(BF16), wpg_ref[...]))
    h = h + gate * _dot(p_ref[...].astype(BF16), wpp_ref[...])
    out_ref[...] = _rms(h, gfin_ref[...])


def _mix_ffn_ple(cx, bg, y_attn, x2, p2, w_conv, g_conv, g_attn, w_out, g_ffn, w_up, w_ffn_conv, w_down,
                 g_ple, w_gate, w_proj, g_final, seq):
    t = x2.shape[0]
    tm = TM_FFN
    row = lambda i: (i, 0)
    halo = lambda i: (jnp.maximum(i * (tm // BF16_ROWS) - 1, 0), 0)

    def resident(a):
        return pl.BlockSpec(a.shape, lambda i: (0, 0), pipeline_mode=pl.Buffered(1))

    weights = (w_conv, g_conv, g_attn, w_out, g_ffn, w_up, w_ffn_conv, w_down, g_ple, w_gate, w_proj, g_final)
    return pl.pallas_call(
        functools.partial(_mix_ffn_ple_kernel, tiles_per_seq=seq // tm),
        grid=(t // tm,),
        in_specs=[
            pl.BlockSpec((tm, CONV_WIDTH), row),
            pl.BlockSpec((BF16_ROWS, CONV_WIDTH), halo),
            pl.BlockSpec((tm, CONV_WIDTH), row),
            pl.BlockSpec((tm, ATTN_WIDTH), row),
            pl.BlockSpec((tm, D_MODEL), row),
            pl.BlockSpec((tm, PLE_DIM), row),
            *[resident(w) for w in weights],
        ],
        out_specs=pl.BlockSpec((tm, D_MODEL), row),
        out_shape=jax.ShapeDtypeStruct((t, D_MODEL), F32),
        scratch_shapes=[pltpu.VMEM((tm, D_FF), BF16), pltpu.VMEM((BF16_ROWS, D_MODEL), BF16)],
        compiler_params=pltpu.CompilerParams(
            dimension_semantics=("arbitrary",), vmem_limit_bytes=VMEM_LIMIT),
        name="mix_ffn_ple",
    )(cx, cx, bg, y_attn, x2, p2, *weights)


def _permute_in_proj(w_in):
    per_kv = GROUP_SIZE * N_BRANCH
    assert w_in.shape[1] == COL_GATE + N_KV_HEADS * per_kv
    pad = jnp.zeros((w_in.shape[0], LANES - per_kv), w_in.dtype)
    slabs = [w_in[:, :COL_GATE]]
    for h in range(N_KV_HEADS):
        slabs += [w_in[:, COL_GATE + h * per_kv:COL_GATE + (h + 1) * per_kv], pad]
    return jnp.concatenate(slabs, axis=1).astype(BF16)


def _rope_tables(pos):
    inv = ROPE_THETA ** (-jnp.arange(0, HEAD_DIM, 2, dtype=F32) / HEAD_DIM)
    ang = pos.astype(F32)[:, None] * inv[None, :]
    cos = jnp.concatenate([jnp.cos(ang), jnp.cos(ang)], axis=-1)
    sin = jnp.concatenate([-jnp.sin(ang), jnp.sin(ang)], axis=-1)
    return cos, sin


def _compress_weights(w1_k, w1_v, w2_k, w2_v, pe_k, pe_v):
    n_str = 2 * N_KV_HEADS
    half = CMP_LEN // 2
    w1 = jnp.stack([w1_k, w1_k, w1_v, w1_v]).astype(BF16)
    w1 = w1.reshape(n_str, 2, half, HEAD_DIM, HEAD_DIM)
    eye = jnp.eye(n_str, dtype=BF16)
    w1_big = jnp.einsum('chrde,cx->rcdhxe', w1, eye).reshape(half * n_str * HEAD_DIM, 2 * n_str * HEAD_DIM)
    pe = jnp.stack([pe_k, pe_k, pe_v, pe_v]).reshape(n_str, 2, half, HEAD_DIM)
    pe_big = jnp.transpose(pe, (1, 2, 0, 3)).reshape(2, half * n_str * HEAD_DIM)
    pe_big = jnp.concatenate([pe_big, jnp.zeros((6, pe_big.shape[1]), F32)], axis=0)
    w2 = jnp.stack([w2_k, w2_k, w2_v, w2_v])
    out_pos = np.asarray([0, 4, 2, 6])
    place = jnp.zeros((n_str, 2 * n_str), F32).at[np.arange(n_str), out_pos].set(1.0)
    w2_big = jnp.einsum('cde,cx->cdxe', w2, place).reshape(n_str * HEAD_DIM, 2 * n_str * HEAD_DIM)
    return w1_big.astype(BF16), pe_big.astype(BF16), w2_big.astype(BF16)


def kernel(x, p, g_mix, w_in, w_conv_mix, cmp_pe_k, cmp_w1_k, cmp_w2_k, cmp_pe_v, cmp_w1_v, cmp_w2_v,
           g_gn_conv, g_gn_attn, w_out, g_ffn, w_up, w_ffn_conv, w_down, g_ple, w_ple_gate, w_ple_proj,
           g_final):
    batch, seq, _ = x.shape
    depth = w_in.shape[0]
    t = batch * seq
    n_cmp_pad = seq // CMP_STRIDE
    n_blk = seq // SEL_BLOCK

    cos, sin = _rope_tables(jnp.arange(seq, dtype=jnp.int32))
    cos_q = jnp.concatenate([cos, cos], axis=-1)
    sin_q = jnp.concatenate([sin, sin], axis=-1)
    cmp_start = jnp.arange(n_cmp_pad) * CMP_STRIDE
    cos_e, sin_e = _rope_tables(cmp_start + CMP_LEN - 1)
    ones = jnp.ones_like(cos_e)
    zeros = jnp.zeros_like(sin_e)
    reps = CMP_ROWS // n_cmp_pad
    cos_c = jnp.tile(jnp.concatenate([cos_e, ones, ones, ones] * N_KV_HEADS, axis=-1), (reps, 1))
    sin_c = jnp.tile(jnp.concatenate([sin_e, zeros, zeros, zeros] * N_KV_HEADS, axis=-1), (reps, 1))
    blk_start = jnp.arange(n_blk) * SEL_BLOCK
    n_cmp = (seq - CMP_LEN) // CMP_STRIDE + 1
    ovl = (jnp.clip(jnp.minimum(cmp_start[:, None] + CMP_LEN, blk_start[None, :] + SEL_BLOCK)
                    - jnp.maximum(cmp_start[:, None], blk_start[None, :]), 0, None).astype(F32) / CMP_LEN)
    ovl = jnp.where(jnp.arange(n_cmp_pad)[:, None] < n_cmp, ovl, 0.0)
    ovl_t = ovl.T.astype(BF16)
    lane = jnp.arange(LANES)[None, :]
    blk_hot = (lane - HEAD_DIM == jnp.arange(seq)[:, None] // SEL_BLOCK).astype(F32)
    qi = jnp.arange(Q_BLOCK)[None, :, None]
    n_pos = KEY_STEP // Q_BLOCK
    tri = jnp.where(jnp.arange(KEY_STEP)[None, None, :] <= jnp.arange(n_pos)[:, None, None] * Q_BLOCK + qi,
                    0.0, NEG).astype(F32)
    edge = jnp.where(jnp.arange(Q_BLOCK)[None, :] > jnp.arange(Q_BLOCK)[:, None], 0.0, NEG).astype(F32)

    g_lane = jnp.arange(LANES)[:, None]
    out_head = jnp.arange(GROUP_SIZE * HEAD_DIM)[None, :] // HEAD_DIM
    gx_cmp = (g_lane == out_head * N_BRANCH).astype(BF16)
    slab_lane = jnp.arange(N_KV_HEADS * LANES)[:, None]
    out_h = jnp.arange(ATTN_WIDTH)[None, :] // HEAD_DIM
    src_lane = (out_h // GROUP_SIZE) * LANES + (out_h % GROUP_SIZE) * N_BRANCH
    gx_att = jnp.concatenate([(slab_lane == src_lane + r).astype(BF16) for r in (1, 2)], axis=1)

    pad_rows = lambda w: jnp.concatenate([w, jnp.zeros((8 - w.shape[0], w.shape[1]), w.dtype)], axis=0)

    h = x.reshape(t, D_MODEL)
    for i in range(depth):
        w_perm = _permute_in_proj(w_in[i])
        cx, bg, q, kvc, kv8, gates = _inproj(h, g_mix[i][None, :], w_perm, cos_q, sin_q, blk_hot, seq)

        w1_big, pe_big, w2_big = _compress_weights(
            cmp_w1_k[i], cmp_w1_v[i], cmp_w2_k[i], cmp_w2_v[i], cmp_pe_k[i], cmp_pe_v[i])
        r = kvc.reshape(t // CMP_STRIDE, CMP_STRIDE * 2 * KV_WIDTH)
        kcv = _compress(r, w1_big, pe_big, w2_big, cos_c, sin_c)

        qa, oc = _nsa_select(q, gates, kcv, ovl_t, gx_cmp, batch, seq)
        y_attn = _nsa_attend(qa, gates, oc, kv8, tri, edge, gx_att, batch, seq)

        assert depth == 1
        h = _mix_ffn_ple(cx, bg, y_attn, h, p[i].reshape(t, PLE_DIM), pad_rows(w_conv_mix[i]),
                         g_gn_conv[i][None, :], g_gn_attn[i][None, :], w_out[i].astype(BF16), g_ffn[i][None, :],
                         w_up[i].astype(BF16), pad_rows(w_ffn_conv[i]), w_down[i].astype(BF16),
                         g_ple[i][None, :], w_ple_gate[i].astype(BF16), w_ple_proj[i].astype(BF16),
                         g_final[None, :], seq)
    return h.reshape(batch, seq, D_MODEL)
```

```python
import functools

import numpy as np
import jax
import jax.numpy as jnp
from jax import lax
from jax.experimental import pallas as pl
from jax.experimental.pallas import tpu as pltpu

F32 = jnp.float32
BF16 = jnp.bfloat16

D_MODEL = 1024
PLE_DIM = 256
CONV_WIDTH = 512
N_HEADS = 8
N_KV_HEADS = 2
HEAD_DIM = 64
HALF_DIM = HEAD_DIM // 2
GROUP_SIZE = N_HEADS // N_KV_HEADS
ATTN_WIDTH = N_HEADS * HEAD_DIM
KV_WIDTH = N_KV_HEADS * HEAD_DIM
N_BRANCH = 3
CMP_LEN = 32
CMP_STRIDE = 16
SEL_BLOCK = 64
N_SEL = 8
WINDOW = 512
Q_BLOCK = 128
D_FF = 2816
ROPE_THETA = 10000.0
EPS = 1e-6
NEG = -1e30
SEL_FORCE = 1e4

LANES = 128
MXU_COLS = 256
BF16_ROWS = 16
VMEM_LIMIT = 56 * 1024 * 1024

COL_XIN = 0
COL_B = CONV_WIDTH
COL_C = 2 * CONV_WIDTH
COL_Q = 3 * CONV_WIDTH
COL_KVC = COL_Q + ATTN_WIDTH
COL_KV4 = COL_KVC + 2 * KV_WIDTH
COL_GATE = COL_KV4 + 4 * KV_WIDTH
IN_COLS_PAD = COL_GATE + N_KV_HEADS * LANES
KV8_WIDTH = 2 * 2 * N_KV_HEADS * LANES
Q_SCALE = HEAD_DIM ** -0.5 * float(np.log2(np.e))

TM_IN = 1024
TM_FFN = 512
TF_FFN = 256
CMP_ROWS = 512
KEY_STEP = 256
SEL_TILE = 512


def _dot(a, b):
    return jnp.dot(a, b, preferred_element_type=F32)


def _dot_nt(a, b):
    return lax.dot_general(a, b, (((1,), (1,)), ((), ())), preferred_element_type=F32)


def _sigmoid(x):
    return 1.0 / (1.0 + jnp.exp(-x))


def _rms(x, g):
    return x * lax.rsqrt(jnp.mean(x * x, axis=-1, keepdims=True) + EPS) * g


def _rope_slab(y, cos, sin_signed, first_half):
    fwd = pltpu.roll(y, HALF_DIM, 1)
    bwd = pltpu.roll(y, LANES - HALF_DIM, 1)
    return y * cos + jnp.where(first_half, bwd, fwd) * sin_signed


def _shift_rows(a, halo, seq_start):
    halo = jnp.where(seq_start, 0.0, halo)
    h1 = halo[BF16_ROWS - 1:BF16_ROWS, :]
    h2 = halo[BF16_ROWS - 2:BF16_ROWS - 1, :]
    row = lax.broadcasted_iota(jnp.int32, a.shape, 0)
    s1 = jnp.where(row == 0, h1, pltpu.roll(a, 1, 0))
    s2 = jnp.where(row == 0, h2, jnp.where(row == 1, h1, pltpu.roll(a, 2, 0)))
    return s1, s2


def _inproj_kernel(x_ref, g_ref, w_ref, cos_ref, sin_ref, hot_ref,
                   cx_ref, bg_ref, q_ref, kvc_ref, kv8_ref, gate_ref, kvc_tmp_ref):
    n = _rms(x_ref[...], g_ref[...]).astype(BF16)

    def proj(lo, hi):
        return _dot(n, w_ref[:, lo:hi])

    x_in = proj(COL_XIN, COL_B)
    bg_ref[...] = proj(COL_B, COL_C).astype(BF16)
    cx_ref[...] = (proj(COL_C, COL_Q) * x_in).astype(BF16)

    cos = cos_ref[...]
    sin = sin_ref[...]
    lane = lax.broadcasted_iota(jnp.int32, cos.shape, 1)
    first_half = (lane % HEAD_DIM) < HALF_DIM
    for s in range(0, ATTN_WIDTH, MXU_COLS):
        y = proj(COL_Q + s, COL_Q + s + MXU_COLS)
        for lo in range(0, MXU_COLS, LANES):
            q_ref[:, s + lo:s + lo + LANES] = (
                _rope_slab(y[:, lo:lo + LANES], cos, sin, first_half) * Q_SCALE).astype(BF16)

    kvc = proj(COL_KVC, COL_KV4)
    width = 2 * KV_WIDTH
    for s in range(width // LANES):
        kvc_tmp_ref[s] = kvc[:, s * LANES:(s + 1) * LANES]
    for r in range(CMP_STRIDE):
        for s in range(width // LANES):
            kvc_ref[:, r * width + s * LANES:r * width + (s + 1) * LANES] = kvc_tmp_ref[
                s, pl.ds(r, kvc_ref.shape[0], stride=CMP_STRIDE), :].astype(BF16)

    in_head = lane < HEAD_DIM
    ones_col = jnp.where(in_head, 0.0, 1.0)
    zeros = jnp.zeros_like(cos)
    for branch, key_fill in enumerate((hot_ref[...], zeros)):
        kv_pair = proj(COL_KV4 + branch * MXU_COLS, COL_KV4 + (branch + 1) * MXU_COLS)
        k_pair = _rope_slab(kv_pair[:, :LANES], cos, sin, first_half)
        v_pair = kv_pair[:, LANES:]
        for h in range(N_KV_HEADS):
            k_h = k_pair if h == 0 else pltpu.roll(k_pair, HEAD_DIM, 1)
            v_h = v_pair if h == 0 else pltpu.roll(v_pair, HEAD_DIM, 1)
            slab = (branch * N_KV_HEADS + h) * 2
            kv8_ref[:, slab * LANES:(slab + 1) * LANES] = jnp.where(in_head, k_h, key_fill).astype(BF16)
            kv8_ref[:, (slab + 1) * LANES:(slab + 2) * LANES] = jnp.where(in_head, v_h, ones_col).astype(BF16)

    gate_ref[...] = _sigmoid(proj(COL_GATE, IN_COLS_PAD))


def _inproj(x2, g_mix, w_perm, cos_q, sin_q, blk_hot, seq):
    t = x2.shape[0]
    tm = TM_IN
    n_seq_tiles = seq // tm
    row = lambda i: (i, 0)
    fixed = lambda i: (0, 0)
    tab = lambda i: (i % n_seq_tiles, 0)
    return pl.pallas_call(
        _inproj_kernel,
        grid=(t // tm,),
        in_specs=[
            pl.BlockSpec((tm, D_MODEL), row),
            pl.BlockSpec((1, D_MODEL), fixed),
            pl.BlockSpec((D_MODEL, IN_COLS_PAD), fixed),
            pl.BlockSpec((tm, LANES), tab),
            pl.BlockSpec((tm, LANES), tab),
            pl.BlockSpec((tm, LANES), tab),
        ],
        out_specs=[
            pl.BlockSpec((tm, CONV_WIDTH), row),
            pl.BlockSpec((tm, CONV_WIDTH), row),
            pl.BlockSpec((tm, ATTN_WIDTH), row),
            pl.BlockSpec((tm // CMP_STRIDE, CMP_STRIDE * 2 * KV_WIDTH), row),
            pl.BlockSpec((tm, KV8_WIDTH), row),
            pl.BlockSpec((tm, N_KV_HEADS * LANES), row),
        ],
        out_shape=[
            jax.ShapeDtypeStruct((t, CONV_WIDTH), BF16),
            jax.ShapeDtypeStruct((t, CONV_WIDTH), BF16),
            jax.ShapeDtypeStruct((t, ATTN_WIDTH), BF16),
            jax.ShapeDtypeStruct((t // CMP_STRIDE, CMP_STRIDE * 2 * KV_WIDTH), BF16),
            jax.ShapeDtypeStruct((t, KV8_WIDTH), BF16),
            jax.ShapeDtypeStruct((t, N_KV_HEADS * LANES), F32),
        ],
        scratch_shapes=[pltpu.VMEM((2 * KV_WIDTH // LANES, tm, LANES), F32)],
        compiler_params=pltpu.CompilerParams(
            dimension_semantics=("arbitrary",), vmem_limit_bytes=VMEM_LIMIT),
        name="inproj",
    )(x2, g_mix, w_perm, cos_q, sin_q, blk_hot)


def _compress_kernel(r_ref, w1_ref, pe_ref, w2_ref, cos_ref, sin_ref, out_ref):
    width = 2 * KV_WIDTH
    p = _dot(r_ref[...], w1_ref[...])
    pb = _dot(pe_ref[...], w1_ref[...])
    bias = pb[0:1, :width] + pb[1:2, width:]
    rows = p.shape[0]
    nxt = pltpu.roll(p[:, width:], rows - 1, 0)
    pre = p[:, :width] + nxt + bias
    hid = (pre * _sigmoid(pre)).astype(BF16)
    y = _dot(hid, w2_ref[...])
    cos = cos_ref[...]
    sin = sin_ref[...]
    lane = lax.broadcasted_iota(jnp.int32, (rows, LANES), 1)
    first_half = (lane % HEAD_DIM) < HALF_DIM
    for s in range(y.shape[1] // LANES):
        sl = slice(s * LANES, (s + 1) * LANES)
        out_ref[:, sl] = _rope_slab(y[:, sl], cos[:, sl], sin[:, sl], first_half).astype(BF16)


def _compress(r, w1_big, pe_big, w2_big, cos_c, sin_c):
    rows = r.shape[0]
    width = w2_big.shape[1]
    row = lambda i: (i, 0)
    fixed = lambda i: (0, 0)
    return pl.pallas_call(
        _compress_kernel,
        grid=(rows // CMP_ROWS,),
        in_specs=[
            pl.BlockSpec((CMP_ROWS, r.shape[1]), row),
            pl.BlockSpec(w1_big.shape, fixed),
            pl.BlockSpec(pe_big.shape, fixed),
            pl.BlockSpec(w2_big.shape, fixed),
            pl.BlockSpec((CMP_ROWS, width), fixed),
            pl.BlockSpec((CMP_ROWS, width), fixed),
        ],
        out_specs=pl.BlockSpec((CMP_ROWS, width), row),
        out_shape=jax.ShapeDtypeStruct((rows, width), BF16),
        compiler_params=pltpu.CompilerParams(
            dimension_semantics=("arbitrary",), vmem_limit_bytes=VMEM_LIMIT),
        name="compress",
    )(r, w1_big, pe_big, w2_big, cos_c, sin_c)


def _stack_heads(q, lane):
    parts = []
    for g in range(GROUP_SIZE):
        slab = q[:, (g // 2) * LANES:(g // 2 + 1) * LANES]
        if g % 2 == 1:
            slab = pltpu.roll(slab, HEAD_DIM, 1)
        parts.append(jnp.where(lane < HEAD_DIM, slab, 0.0))
    return parts


def _merge_head_pair(even, odd, lane):
    return jnp.where(lane < HEAD_DIM, even, pltpu.roll(odd, HEAD_DIM, 1))


def _expand_gates(gate, expand):
    hi = gate.astype(BF16)
    lo = (gate - hi.astype(F32)).astype(BF16)
    return _dot(hi, expand) + _dot(lo, expand)


def _nsa_select_kernel(q_ref, gate_ref, kc_ref, vc_ref, ovl_ref, gx_ref, qa_ref, oc_ref):
    tile = q_ref.shape[0]
    q0 = pl.program_id(2) * tile
    lane = lax.broadcasted_iota(jnp.int32, (tile, LANES), 1)
    parts = _stack_heads(q_ref[...].astype(F32), lane)
    qs = jnp.concatenate(parts, axis=0).astype(BF16)

    n_cmp = kc_ref.shape[0]
    t_row = q0 + lax.broadcasted_iota(jnp.int32, (tile, n_cmp), 0)
    end_c = lax.broadcasted_iota(jnp.int32, (tile, n_cmp), 1) * CMP_STRIDE + (CMP_LEN - 1)
    bias = jnp.where(end_c <= t_row, 0.0, NEG)
    s_c = _dot_nt(qs, kc_ref[...]) + jnp.concatenate([bias] * GROUP_SIZE, axis=0)
    e_c = jnp.exp2(s_c - jnp.max(s_c, axis=-1, keepdims=True))
    p_c = e_c / jnp.maximum(jnp.sum(e_c, axis=-1, keepdims=True), 1e-30)
    live = t_row[:CMP_LEN, :] >= CMP_LEN - 1
    p_c = jnp.concatenate(
        [piece for g in range(GROUP_SIZE)
         for piece in (jnp.where(live, p_c[g * tile:g * tile + CMP_LEN], 0.0),
                       p_c[g * tile + CMP_LEN:(g + 1) * tile])],
        axis=0)
    o_c = _dot(p_c.astype(BF16), vc_ref[...])

    psum = p_c[0:tile]
    for g in range(1, GROUP_SIZE):
        psum = psum + p_c[g * tile:(g + 1) * tile]
    p_hi = psum.astype(BF16)
    p_lo = (psum - p_hi.astype(F32)).astype(BF16)
    ovl = ovl_ref[...]
    imp = _dot_nt(ovl, p_hi) + _dot_nt(ovl, p_lo)
    n_blk = imp.shape[0]
    blk = lax.broadcasted_iota(jnp.int32, imp.shape, 0)
    t_q = q0 + lax.broadcasted_iota(jnp.int32, imp.shape, 1)
    cur = t_q // SEL_BLOCK
    valid = blk * SEL_BLOCK <= t_q
    forced = (blk == 0) | (blk == cur) | (blk == cur - 1)
    score = jnp.where(valid, imp + jnp.where(forced, SEL_FORCE, 0.0), NEG)
    sub = 8
    groups = [score[r:r + sub] for r in range(0, n_blk, sub)]
    ranks = [jnp.zeros_like(g) for g in groups]
    row = lax.broadcasted_iota(jnp.int32, groups[0].shape, 0)
    for m in range(n_blk):
        other = score[m:m + 1, :]
        for r, g in enumerate(groups):
            if r < m // sub:
                ranks[r] = ranks[r] + jnp.where(other > g, 1.0, 0.0)
            elif r > m // sub:
                ranks[r] = ranks[r] + jnp.where(other >= g, 1.0, 0.0)
            else:
                tie = jnp.where(row > m % sub, 1.0, 0.0)
                ranks[r] = ranks[r] + jnp.where(other > g, 1.0, 0.0) + jnp.where(other == g, tie, 0.0)
    rank = jnp.concatenate(ranks, axis=0)
    sel_neg = jnp.where((rank < float(min(N_SEL, n_blk))) & valid, 0.0, NEG)
    pad_lo = jnp.zeros((HEAD_DIM, tile), F32)
    pad_hi = jnp.zeros((LANES - HEAD_DIM - n_blk, tile), F32)
    sel_rows = jnp.concatenate([pad_lo, sel_neg, pad_hi], axis=0).T
    for g in range(GROUP_SIZE):
        qa_ref[:, g * LANES:(g + 1) * LANES] = (parts[g] + sel_rows).astype(BF16)

    gate = _expand_gates(gate_ref[...], gx_ref[...])
    for j in range(GROUP_SIZE // 2):
        pair = _merge_head_pair(o_c[2 * j * tile:(2 * j + 1) * tile], o_c[(2 * j + 1) * tile:(2 * j + 2) * tile], lane)
        oc_ref[:, j * LANES:(j + 1) * LANES] = (gate[:, j * LANES:(j + 1) * LANES] * pair).astype(BF16)


def _nsa_select(q, gates, kcv, ovl_t, gate_expand, batch, seq):
    tile = SEL_TILE
    n_t = seq // tile
    n_cmp_pad = kcv.shape[0] // batch
    grp = GROUP_SIZE * HEAD_DIM
    q_map = lambda b, h, i: (b * n_t + i, h)
    return pl.pallas_call(
        _nsa_select_kernel,
        grid=(batch, N_KV_HEADS, n_t),
        in_specs=[
            pl.BlockSpec((tile, grp), q_map),
            pl.BlockSpec((tile, LANES), q_map),
            pl.BlockSpec((n_cmp_pad, LANES), lambda b, h, i: (b, 2 * h)),
            pl.BlockSpec((n_cmp_pad, LANES), lambda b, h, i: (b, 2 * h + 1)),
            pl.BlockSpec(ovl_t.shape, lambda b, h, i: (0, 0)),
            pl.BlockSpec(gate_expand.shape, lambda b, h, i: (0, 0)),
        ],
        out_specs=[pl.BlockSpec((tile, GROUP_SIZE * LANES), q_map), pl.BlockSpec((tile, grp), q_map)],
        out_shape=[jax.ShapeDtypeStruct((batch * seq, N_HEADS * LANES), BF16),
                   jax.ShapeDtypeStruct((batch * seq, ATTN_WIDTH), BF16)],
        compiler_params=pltpu.CompilerParams(
            dimension_semantics=("arbitrary", "arbitrary", "arbitrary"), vmem_limit_bytes=VMEM_LIMIT),
        name="nsa_select",
    )(q, gates, kcv, kcv, ovl_t, gate_expand)


def _nsa_attend_kernel(qa_ref, gate_ref, oc_ref, *refs):
    i = pl.program_id(1)
    seq = refs[0].shape[0]
    for pair in range(seq // KEY_STEP):
        pl.when(i // (KEY_STEP // Q_BLOCK) == pair)(functools.partial(
            _nsa_attend_body, pair, qa_ref, gate_ref, oc_ref, *refs))


def _nsa_attend_body(pair, qa_ref, gate_ref, oc_ref, *refs):
    kv_refs, (tri_ref, edge_ref, gx_ref, o_ref) = refs[:-4], refs[-4:]
    i = pl.program_id(1)
    q0 = i * Q_BLOCK
    lane = lax.broadcasted_iota(jnp.int32, (Q_BLOCK, LANES), 1)
    k_end = (pair + 1) * KEY_STEP
    windowed = k_end > Q_BLOCK + WINDOW
    n_win = Q_BLOCK + WINDOW if windowed else k_end
    w0 = pl.multiple_of(q0 - WINDOW, Q_BLOCK) if windowed else 0
    tri = jnp.concatenate([tri_ref[i % (KEY_STEP // Q_BLOCK)]] * 2, axis=0)
    diag = tri_ref[KEY_STEP // Q_BLOCK - 1][:, KEY_STEP - Q_BLOCK:]
    diag = jnp.concatenate([diag] * 2, axis=0)
    edge = jnp.concatenate([edge_ref[...]] * 2, axis=0)
    gexp = _expand_gates(gate_ref[...], gx_ref[...])
    gates = [gexp[:, :ATTN_WIDTH], gexp[:, ATTN_WIDTH:]]

    def causal(s):
        past = s.shape[1] - KEY_STEP
        return jnp.concatenate([s[:, :past], s[:, past:] + tri], axis=1) if past else s + tri

    def queries(h, j):
        return jnp.concatenate(
            [qa_ref[:, (h * GROUP_SIZE + g) * LANES:(h * GROUP_SIZE + g + 1) * LANES] for g in (2 * j, 2 * j + 1)],
            axis=0)

    def score_mats(branch, h, j):
        ks_ref, _, kw_ref, _ = kv_refs[4 * h:4 * h + 4]
        if branch == "sel":
            return causal(_dot_nt(queries(h, j), ks_ref[0:k_end, :]))
        s = _dot_nt(queries(h, j), kw_ref[pl.ds(w0, n_win), :])
        if not windowed:
            return causal(s)
        return jnp.concatenate([s[:, :Q_BLOCK] + edge, s[:, Q_BLOCK:WINDOW], s[:, WINDOW:] + diag], axis=1)

    def numerators(s):
        return jnp.exp2(s - jnp.max(s, axis=-1, keepdims=True)).astype(BF16)

    def weighted(branch, h, probs):
        _, vs_ref, _, vw_ref = kv_refs[4 * h:4 * h + 4]
        return _dot(probs, vs_ref[0:k_end, :] if branch == "sel" else vw_ref[pl.ds(w0, n_win), :])

    def normalised(acc, gate):
        even, odd = acc[:Q_BLOCK], acc[Q_BLOCK:]
        num = jnp.where(lane < HEAD_DIM, even, pltpu.roll(odd, HEAD_DIM, 1))
        den = jnp.where(lane < HEAD_DIM, pltpu.roll(even, HEAD_DIM, 1), odd)
        return num * (gate / jnp.maximum(den, 1e-30))

    def merge(h, j, acc_w, acc_s):
        cols = slice((h * (GROUP_SIZE // 2) + j) * LANES, (h * (GROUP_SIZE // 2) + j + 1) * LANES)
        merged = (oc_ref[:, cols].astype(F32) + normalised(acc_s, gates[0][:, cols])
                  + normalised(acc_w, gates[1][:, cols]))
        o_ref[:, cols] = merged.astype(BF16)

    chains = [(branch, h, j) for h in range(N_KV_HEADS) for j in range(GROUP_SIZE // 2)
              for branch in ("win", "sel")]
    scores, probs, outs = {}, {}, {}
    for step in range(len(chains) + 2):
        if step < len(chains):
            scores[step] = score_mats(*chains[step])
        if 0 <= step - 1 < len(chains):
            probs[step - 1] = numerators(scores.pop(step - 1))
        if 0 <= step - 2 < len(chains):
            branch, h, j = chains[step - 2]
            outs[branch] = weighted(branch, h, probs.pop(step - 2))
            if branch == "sel":
                merge(h, j, outs.pop("win"), outs.pop("sel"))


def _nsa_attend(qa, gates, oc, kv8, tri, edge, gate_expand, batch, seq):
    n_qb = seq // Q_BLOCK
    q_map = lambda b, i: (b * n_qb + i, 0)
    fixed = lambda b, i: (0, 0)
    fixed3 = lambda b, i: (0, 0, 0)

    def kv_spec(slab):
        return pl.BlockSpec((seq, LANES), lambda b, i: (b, slab))

    kv_slabs = [branch * 2 * N_KV_HEADS + 2 * h + part
                for h in range(N_KV_HEADS) for branch in range(2) for part in range(2)]
    return pl.pallas_call(
        _nsa_attend_kernel,
        grid=(batch, n_qb),
        in_specs=[
            pl.BlockSpec((Q_BLOCK, N_HEADS * LANES), q_map),
            pl.BlockSpec((Q_BLOCK, N_KV_HEADS * LANES), q_map),
            pl.BlockSpec((Q_BLOCK, ATTN_WIDTH), q_map),
            *[kv_spec(s) for s in kv_slabs],
            pl.BlockSpec(tri.shape, fixed3),
            pl.BlockSpec(edge.shape, fixed),
            pl.BlockSpec(gate_expand.shape, fixed),
        ],
        out_specs=pl.BlockSpec((Q_BLOCK, ATTN_WIDTH), q_map),
        out_shape=jax.ShapeDtypeStruct((batch * seq, ATTN_WIDTH), BF16),
        compiler_params=pltpu.CompilerParams(
            dimension_semantics=("arbitrary", "arbitrary"), vmem_limit_bytes=VMEM_LIMIT),
        name="nsa_attend",
    )(qa, gates, oc, *([kv8] * len(kv_slabs)), tri, edge, gate_expand)


def _mix_ffn_ple_kernel(cx_ref, cxh_ref, bg_ref, y_ref, x_ref, p_ref, wc_ref, gc_ref, ga_ref, wo_ref, gf_ref,
                        wu_ref, cu_ref, wd_ref, gp_ref, wpg_ref, wpp_ref, gfin_ref,
                        out_ref, act_ref, n_prev_ref, *, tiles_per_seq):
    step = pl.program_id(0)
    seq_start = step % tiles_per_seq == 0

    @pl.when(step == 0)
    def _():
        n_prev_ref[...] = jnp.zeros_like(n_prev_ref)

    cx = cx_ref[...].astype(F32)
    s1, s2 = _shift_rows(cx, cxh_ref[...].astype(F32), seq_start)
    wc = wc_ref[...]
    y_conv = bg_ref[...].astype(F32) * (wc[0:1] * s2 + wc[1:2] * s1 + wc[2:3] * cx)
    mixed = jnp.concatenate(
        [_rms(y_conv, gc_ref[...]).astype(BF16), _rms(y_ref[...].astype(F32), ga_ref[...]).astype(BF16)], axis=1)
    h = x_ref[...] + _dot(mixed, wo_ref[...])

    n = _rms(h, gf_ref[...]).astype(BF16)
    nh = n_prev_ref[...]
    n_prev_ref[...] = n[n.shape[0] - BF16_ROWS:, :]

    n_ext = jnp.concatenate([nh, n], axis=0)

    def conv_up(lo):
        a_ext = _dot(n_ext, wu_ref[:, lo:lo + TF_FFN])
        a = a_ext[BF16_ROWS:]
        t1, t2 = _shift_rows(a, a_ext[:BF16_ROWS], seq_start)
        c = cu_ref[:, lo:lo + TF_FFN]
        return c[0:1] * t2 + c[1:2] * t1 + c[2:3] * a

    for j in range(D_FF // TF_FFN):
        u_gate = conv_up(j * TF_FFN)
        u_val = conv_up(D_FF + j * TF_FFN)
        act_ref[:, j * TF_FFN:(j + 1) * TF_FFN] = (u_gate * _sigmoid(u_gate) * u_val).astype(BF16)
    h = h + _dot(act_ref[...], wd_ref[...])

    gate = _sigmoid(_dot(_rms(h, gp_ref[...]).astype(BF16), wpg_ref[...]))
    h = h + gate * _dot(p_ref[...].astype(BF16), wpp_ref[...])
    out_ref[...] = _rms(h, gfin_ref[...])


def _mix_ffn_ple(cx, bg, y_attn, x2, p2, w_conv, g_conv, g_attn, w_out, g_ffn, w_up, w_ffn_conv, w_down,
                 g_ple, w_gate, w_proj, g_final, seq):
    t = x2.shape[0]
    tm = TM_FFN
    row = lambda i: (i, 0)
    halo = lambda i: (jnp.maximum(i * (tm // BF16_ROWS) - 1, 0), 0)

    def resident(a):
        return pl.BlockSpec(a.shape, lambda i: (0, 0), pipeline_mode=pl.Buffered(1))

    weights = (w_conv, g_conv, g_attn, w_out, g_ffn, w_up, w_ffn_conv, w_down, g_ple, w_gate, w_proj, g_final)
    return pl.pallas_call(
        functools.partial(_mix_ffn_ple_kernel, tiles_per_seq=seq // tm),
        grid=(t // tm,),
        in_specs=[
            pl.BlockSpec((tm, CONV_WIDTH), row),
            pl.BlockSpec((BF16_ROWS, CONV_WIDTH), halo),
            pl.BlockSpec((tm, CONV_WIDTH), row),
            pl.BlockSpec((tm, ATTN_WIDTH), row),
            pl.BlockSpec((tm, D_MODEL), row),
            pl.BlockSpec((tm, PLE_DIM), row),
            *[resident(w) for w in weights],
        ],
        out_specs=pl.BlockSpec((tm, D_MODEL), row),
        out_shape=jax.ShapeDtypeStruct((t, D_MODEL), F32),
        scratch_shapes=[pltpu.VMEM((tm, D_FF), BF16), pltpu.VMEM((BF16_ROWS, D_MODEL), BF16)],
        compiler_params=pltpu.CompilerParams(
            dimension_semantics=("arbitrary",), vmem_limit_bytes=VMEM_LIMIT),
        name="mix_ffn_ple",
    )(cx, cx, bg, y_attn, x2, p2, *weights)


def _permute_in_proj(w_in):
    per_kv = GROUP_SIZE * N_BRANCH
    assert w_in.shape[1] == COL_GATE + N_KV_HEADS * per_kv
    pad = jnp.zeros((w_in.shape[0], LANES - per_kv), w_in.dtype)
    slabs = [w_in[:, :COL_GATE]]
    for h in range(N_KV_HEADS):
        slabs += [w_in[:, COL_GATE + h * per_kv:COL_GATE + (h + 1) * per_kv], pad]
    return jnp.concatenate(slabs, axis=1).astype(BF16)


def _rope_tables(pos):
    inv = ROPE_THETA ** (-jnp.arange(0, HEAD_DIM, 2, dtype=F32) / HEAD_DIM)
    ang = pos.astype(F32)[:, None] * inv[None, :]
    cos = jnp.concatenate([jnp.cos(ang), jnp.cos(ang)], axis=-1)
    sin = jnp.concatenate([-jnp.sin(ang), jnp.sin(ang)], axis=-1)
    return cos, sin


def _compress_weights(w1_k, w1_v, w2_k, w2_v, pe_k, pe_v):
    n_str = 2 * N_KV_HEADS
    half = CMP_LEN // 2
    w1 = jnp.stack([w1_k, w1_k, w1_v, w1_v]).astype(BF16)
    w1 = w1.reshape(n_str, 2, half, HEAD_DIM, HEAD_DIM)
    eye = jnp.eye(n_str, dtype=BF16)
    w1_big = jnp.einsum('chrde,cx->rcdhxe', w1, eye).reshape(half * n_str * HEAD_DIM, 2 * n_str * HEAD_DIM)
    pe = jnp.stack([pe_k, pe_k, pe_v, pe_v]).reshape(n_str, 2, half, HEAD_DIM)
    pe_big = jnp.transpose(pe, (1, 2, 0, 3)).reshape(2, half * n_str * HEAD_DIM)
    pe_big = jnp.concatenate([pe_big, jnp.zeros((6, pe_big.shape[1]), F32)], axis=0)
    w2 = jnp.stack([w2_k, w2_k, w2_v, w2_v])
    out_pos = np.asarray([0, 4, 2, 6])
    place = jnp.zeros((n_str, 2 * n_str), F32).at[np.arange(n_str), out_pos].set(1.0)
    w2_big = jnp.einsum('cde,cx->cdxe', w2, place).reshape(n_str * HEAD_DIM, 2 * n_str * HEAD_DIM)
    return w1_big.astype(BF16), pe_big.astype(BF16), w2_big.astype(BF16)


def kernel(x, p, g_mix, w_in, w_conv_mix, cmp_pe_k, cmp_w1_k, cmp_w2_k, cmp_pe_v, cmp_w1_v, cmp_w2_v,
           g_gn_conv, g_gn_attn, w_out, g_ffn, w_up, w_ffn_conv, w_down, g_ple, w_ple_gate, w_ple_proj,
           g_final):
    batch, seq, _ = x.shape
    depth = w_in.shape[0]
    t = batch * seq
    n_cmp_pad = seq // CMP_STRIDE
    n_blk = seq // SEL_BLOCK

    cos, sin = _rope_tables(jnp.arange(seq, dtype=jnp.int32))
    cos_q = jnp.concatenate([cos, cos], axis=-1)
    sin_q = jnp.concatenate([sin, sin], axis=-1)
    cmp_start = jnp.arange(n_cmp_pad) * CMP_STRIDE
    cos_e, sin_e = _rope_tables(cmp_start + CMP_LEN - 1)
    ones = jnp.ones_like(cos_e)
    zeros = jnp.zeros_like(sin_e)
    reps = CMP_ROWS // n_cmp_pad
    cos_c = jnp.tile(jnp.concatenate([cos_e, ones, ones, ones] * N_KV_HEADS, axis=-1), (reps, 1))
    sin_c = jnp.tile(jnp.concatenate([sin_e, zeros, zeros, zeros] * N_KV_HEADS, axis=-1), (reps, 1))
    blk_start = jnp.arange(n_blk) * SEL_BLOCK
    n_cmp = (seq - CMP_LEN) // CMP_STRIDE + 1
    ovl = (jnp.clip(jnp.minimum(cmp_start[:, None] + CMP_LEN, blk_start[None, :] + SEL_BLOCK)
                    - jnp.maximum(cmp_start[:, None], blk_start[None, :]), 0, None).astype(F32) / CMP_LEN)
    ovl = jnp.where(jnp.arange(n_cmp_pad)[:, None] < n_cmp, ovl, 0.0)
    ovl_t = ovl.T.astype(BF16)
    lane = jnp.arange(LANES)[None, :]
    blk_hot = (lane - HEAD_DIM == jnp.arange(seq)[:, None] // SEL_BLOCK).astype(F32)
    qi = jnp.arange(Q_BLOCK)[None, :, None]
    n_pos = KEY_STEP // Q_BLOCK
    tri = jnp.where(jnp.arange(KEY_STEP)[None, None, :] <= jnp.arange(n_pos)[:, None, None] * Q_BLOCK + qi,
                    0.0, NEG).astype(F32)
    edge = jnp.where(jnp.arange(Q_BLOCK)[None, :] > jnp.arange(Q_BLOCK)[:, None], 0.0, NEG).astype(F32)

    g_lane = jnp.arange(LANES)[:, None]
    out_head = jnp.arange(GROUP_SIZE * HEAD_DIM)[None, :] // HEAD_DIM
    gx_cmp = (g_lane == out_head * N_BRANCH).astype(BF16)
    slab_lane = jnp.arange(N_KV_HEADS * LANES)[:, None]
    out_h = jnp.arange(ATTN_WIDTH)[None, :] // HEAD_DIM
    src_lane = (out_h // GROUP_SIZE) * LANES + (out_h % GROUP_SIZE) * N_BRANCH
    gx_att = jnp.concatenate([(slab_lane == src_lane + r).astype(BF16) for r in (1, 2)], axis=1)

    pad_rows = lambda w: jnp.concatenate([w, jnp.zeros((8 - w.shape[0], w.shape[1]), w.dtype)], axis=0)

    h = x.reshape(t, D_MODEL)
    for i in range(depth):
        w_perm = _permute_in_proj(w_in[i])
        cx, bg, q, kvc, kv8, gates = _inproj(h, g_mix[i][None, :], w_perm, cos_q, sin_q, blk_hot, seq)

        w1_big, pe_big, w2_big = _compress_weights(
            cmp_w1_k[i], cmp_w1_v[i], cmp_w2_k[i], cmp_w2_v[i], cmp_pe_k[i], cmp_pe_v[i])
        kcv = _compress(kvc, w1_big, pe_big, w2_big, cos_c, sin_c)

        qa, oc = _nsa_select(q, gates, kcv, ovl_t, gx_cmp, batch, seq)
        y_attn = _nsa_attend(qa, gates, oc, kv8, tri, edge, gx_att, batch, seq)

        assert depth == 1
        h = _mix_ffn_ple(cx, bg, y_attn, h, p[i].reshape(t, PLE_DIM), pad_rows(w_conv_mix[i]),
                         g_gn_conv[i][None, :], g_gn_attn[i][None, :], w_out[i].astype(BF16), g_ffn[i][None, :],
                         w_up[i].astype(BF16), pad_rows(w_ffn_conv[i]), w_down[i].astype(BF16),
                         g_ple[i][None, :], w_ple_gate[i].astype(BF16), w_ple_proj[i].astype(BF16),
                         g_final[None, :], seq)
    return h.reshape(batch, seq, D_MODEL)
```

```python
import functools

import numpy as np
import jax
import jax.numpy as jnp
from jax import lax
from jax.experimental import pallas as pl
from jax.experimental.pallas import tpu as pltpu

F32 = jnp.float32
BF16 = jnp.bfloat16

D_MODEL = 1024
PLE_DIM = 256
CONV_WIDTH = 512
N_HEADS = 8
N_KV_HEADS = 2
HEAD_DIM = 64
HALF_DIM = HEAD_DIM // 2
GROUP_SIZE = N_HEADS // N_KV_HEADS
ATTN_WIDTH = N_HEADS * HEAD_DIM
KV_WIDTH = N_KV_HEADS * HEAD_DIM
N_BRANCH = 3
CMP_LEN = 32
CMP_STRIDE = 16
SEL_BLOCK = 64
N_SEL = 8
WINDOW = 512
Q_BLOCK = 128
D_FF = 2816
ROPE_THETA = 10000.0
EPS = 1e-6
NEG = -1e30
SEL_FORCE = 1e4

LANES = 128
MXU_COLS = 256
BF16_ROWS = 16
VMEM_LIMIT = 56 * 1024 * 1024

COL_XIN = 0
COL_B = CONV_WIDTH
COL_C = 2 * CONV_WIDTH
COL_Q = 3 * CONV_WIDTH
COL_KVC = COL_Q + ATTN_WIDTH
COL_KV4 = COL_KVC + 2 * KV_WIDTH
COL_GATE = COL_KV4 + 4 * KV_WIDTH
KV8_WIDTH = 2 * 2 * N_KV_HEADS * LANES
Q_SCALE = HEAD_DIM ** -0.5 * float(np.log2(np.e))

TM_IN = 1024
TM_FFN = 512
TF_FFN = 256
CMP_ROWS = 512
KEY_STEP = 256
SEL_TILE = 512


def _dot(a, b):
    return jnp.dot(a, b, preferred_element_type=F32)


def _dot_nt(a, b):
    return lax.dot_general(a, b, (((1,), (1,)), ((), ())), preferred_element_type=F32)


def _sigmoid(x):
    return 1.0 / (1.0 + jnp.exp(-x))


def _rms(x, g):
    return x * lax.rsqrt(jnp.mean(x * x, axis=-1, keepdims=True) + EPS) * g


def _rope_slab(y, cos, sin_signed, first_half):
    fwd = pltpu.roll(y, HALF_DIM, 1)
    bwd = pltpu.roll(y, LANES - HALF_DIM, 1)
    return y * cos + jnp.where(first_half, bwd, fwd) * sin_signed


def _shift_rows(a, halo, seq_start):
    halo = jnp.where(seq_start, 0.0, halo)
    h1 = halo[BF16_ROWS - 1:BF16_ROWS, :]
    h2 = halo[BF16_ROWS - 2:BF16_ROWS - 1, :]
    row = lax.broadcasted_iota(jnp.int32, a.shape, 0)
    s1 = jnp.where(row == 0, h1, pltpu.roll(a, 1, 0))
    s2 = jnp.where(row == 0, h2, jnp.where(row == 1, h1, pltpu.roll(a, 2, 0)))
    return s1, s2


def _inproj_kernel(x_ref, g_ref, w_ref, wg_ref, cos_ref, sin_ref, hot_ref,
                   cx_ref, bg_ref, q_ref, kvc_ref, kv8_ref, gate_ref, kvc_tmp_ref):
    n = _rms(x_ref[...], g_ref[...]).astype(BF16)

    def proj(lo, hi):
        return _dot(n, w_ref[:, lo:hi])

    x_in = proj(COL_XIN, COL_B)
    bg_ref[...] = proj(COL_B, COL_C).astype(BF16)
    cx_ref[...] = (proj(COL_C, COL_Q) * x_in).astype(BF16)

    cos = cos_ref[...]
    sin = sin_ref[...]
    lane = lax.broadcasted_iota(jnp.int32, cos.shape, 1)
    first_half = (lane % HEAD_DIM) < HALF_DIM
    for s in range(0, ATTN_WIDTH, MXU_COLS):
        y = proj(COL_Q + s, COL_Q + s + MXU_COLS)
        for lo in range(0, MXU_COLS, LANES):
            q_ref[:, s + lo:s + lo + LANES] = (
                _rope_slab(y[:, lo:lo + LANES], cos, sin, first_half) * Q_SCALE).astype(BF16)

    kvc = proj(COL_KVC, COL_KV4)
    width = 2 * KV_WIDTH
    for s in range(width // LANES):
        kvc_tmp_ref[s] = kvc[:, s * LANES:(s + 1) * LANES]
    for r in range(CMP_STRIDE):
        for s in range(width // LANES):
            kvc_ref[:, r * width + s * LANES:r * width + (s + 1) * LANES] = kvc_tmp_ref[
                s, pl.ds(r, kvc_ref.shape[0], stride=CMP_STRIDE), :].astype(BF16)

    in_head = lane < HEAD_DIM
    ones_col = jnp.where(in_head, 0.0, 1.0)
    zeros = jnp.zeros_like(cos)
    for branch, key_fill in enumerate((hot_ref[...], zeros)):
        kv_pair = proj(COL_KV4 + branch * MXU_COLS, COL_KV4 + (branch + 1) * MXU_COLS)
        k_pair = _rope_slab(kv_pair[:, :LANES], cos, sin, first_half)
        v_pair = kv_pair[:, LANES:]
        for h in range(N_KV_HEADS):
            k_h = k_pair if h == 0 else pltpu.roll(k_pair, HEAD_DIM, 1)
            v_h = v_pair if h == 0 else pltpu.roll(v_pair, HEAD_DIM, 1)
            slab = (branch * N_KV_HEADS + h) * 2
            kv8_ref[:, slab * LANES:(slab + 1) * LANES] = jnp.where(in_head, k_h, key_fill).astype(BF16)
            kv8_ref[:, (slab + 1) * LANES:(slab + 2) * LANES] = jnp.where(in_head, v_h, ones_col).astype(BF16)

    gate_ref[...] = _sigmoid(_dot(n, wg_ref[...]))


def _inproj(x2, g_mix, w_main, w_gates, cos_q, sin_q, blk_hot, seq):
    t = x2.shape[0]
    tm = TM_IN
    n_seq_tiles = seq // tm
    row = lambda i: (i, 0)
    fixed = lambda i: (0, 0)
    tab = lambda i: (i % n_seq_tiles, 0)
    return pl.pallas_call(
        _inproj_kernel,
        grid=(t // tm,),
        in_specs=[
            pl.BlockSpec((tm, D_MODEL), row),
            pl.BlockSpec((1, D_MODEL), fixed),
            pl.BlockSpec(w_main.shape, fixed),
            pl.BlockSpec(w_gates.shape, fixed),
            pl.BlockSpec((tm, LANES), tab),
            pl.BlockSpec((tm, LANES), tab),
            pl.BlockSpec((tm, LANES), tab),
        ],
        out_specs=[
            pl.BlockSpec((tm, CONV_WIDTH), row),
            pl.BlockSpec((tm, CONV_WIDTH), row),
            pl.BlockSpec((tm, ATTN_WIDTH), row),
            pl.BlockSpec((tm // CMP_STRIDE, CMP_STRIDE * 2 * KV_WIDTH), row),
            pl.BlockSpec((tm, KV8_WIDTH), row),
            pl.BlockSpec((tm, N_KV_HEADS * LANES), row),
        ],
        out_shape=[
            jax.ShapeDtypeStruct((t, CONV_WIDTH), BF16),
            jax.ShapeDtypeStruct((t, CONV_WIDTH), BF16),
            jax.ShapeDtypeStruct((t, ATTN_WIDTH), BF16),
            jax.ShapeDtypeStruct((t // CMP_STRIDE, CMP_STRIDE * 2 * KV_WIDTH), BF16),
            jax.ShapeDtypeStruct((t, KV8_WIDTH), BF16),
            jax.ShapeDtypeStruct((t, N_KV_HEADS * LANES), F32),
        ],
        scratch_shapes=[pltpu.VMEM((2 * KV_WIDTH // LANES, tm, LANES), F32)],
        compiler_params=pltpu.CompilerParams(
            dimension_semantics=("arbitrary",), vmem_limit_bytes=VMEM_LIMIT),
        name="inproj",
    )(x2, g_mix, w_main, w_gates, cos_q, sin_q, blk_hot)


def _compress_kernel(r_ref, w1_ref, pe_ref, w2_ref, cos_ref, sin_ref, out_ref):
    width = 2 * KV_WIDTH
    p = _dot(r_ref[...], w1_ref[...])
    pb = _dot(pe_ref[...], w1_ref[...])
    bias = pb[0:1, :width] + pb[1:2, width:]
    rows = p.shape[0]
    nxt = pltpu.roll(p[:, width:], rows - 1, 0)
    pre = p[:, :width] + nxt + bias
    hid = (pre * _sigmoid(pre)).astype(BF16)
    y = _dot(hid, w2_ref[...])
    cos = cos_ref[...]
    sin = sin_ref[...]
    lane = lax.broadcasted_iota(jnp.int32, (rows, LANES), 1)
    first_half = (lane % HEAD_DIM) < HALF_DIM
    for s in range(y.shape[1] // LANES):
        sl = slice(s * LANES, (s + 1) * LANES)
        out_ref[:, sl] = _rope_slab(y[:, sl], cos[:, sl], sin[:, sl], first_half).astype(BF16)


def _compress(r, w1_big, pe_big, w2_big, cos_c, sin_c):
    rows = r.shape[0]
    width = w2_big.shape[1]
    row = lambda i: (i, 0)
    fixed = lambda i: (0, 0)
    return pl.pallas_call(
        _compress_kernel,
        grid=(rows // CMP_ROWS,),
        in_specs=[
            pl.BlockSpec((CMP_ROWS, r.shape[1]), row),
            pl.BlockSpec(w1_big.shape, fixed),
            pl.BlockSpec(pe_big.shape, fixed),
            pl.BlockSpec(w2_big.shape, fixed),
            pl.BlockSpec((CMP_ROWS, width), fixed),
            pl.BlockSpec((CMP_ROWS, width), fixed),
        ],
        out_specs=pl.BlockSpec((CMP_ROWS, width), row),
        out_shape=jax.ShapeDtypeStruct((rows, width), BF16),
        compiler_params=pltpu.CompilerParams(
            dimension_semantics=("arbitrary",), vmem_limit_bytes=VMEM_LIMIT),
        name="compress",
    )(r, w1_big, pe_big, w2_big, cos_c, sin_c)


def _stack_heads(q, lane):
    parts = []
    for g in range(GROUP_SIZE):
        slab = q[:, (g // 2) * LANES:(g // 2 + 1) * LANES]
        if g % 2 == 1:
            slab = pltpu.roll(slab, HEAD_DIM, 1)
        parts.append(jnp.where(lane < HEAD_DIM, slab, 0.0))
    return parts


def _merge_head_pair(even, odd, lane):
    return jnp.where(lane < HEAD_DIM, even, pltpu.roll(odd, HEAD_DIM, 1))


def _expand_gates(gate, expand):
    hi = gate.astype(BF16)
    lo = (gate - hi.astype(F32)).astype(BF16)
    return _dot(hi, expand) + _dot(lo, expand)


def _nsa_select_kernel(q_ref, gate_ref, kc_ref, vc_ref, ovl_ref, gx_ref, qa_ref, oc_ref):
    tile = q_ref.shape[0]
    q0 = pl.program_id(2) * tile
    lane = lax.broadcasted_iota(jnp.int32, (tile, LANES), 1)
    parts = _stack_heads(q_ref[...].astype(F32), lane)
    qs = jnp.concatenate(parts, axis=0).astype(BF16)

    n_cmp = kc_ref.shape[0]
    t_row = q0 + lax.broadcasted_iota(jnp.int32, (tile, n_cmp), 0)
    end_c = lax.broadcasted_iota(jnp.int32, (tile, n_cmp), 1) * CMP_STRIDE + (CMP_LEN - 1)
    bias = jnp.where(end_c <= t_row, 0.0, NEG)
    s_c = _dot_nt(qs, kc_ref[...]) + jnp.concatenate([bias] * GROUP_SIZE, axis=0)
    e_c = jnp.exp2(s_c - jnp.max(s_c, axis=-1, keepdims=True))
    p_c = e_c / jnp.maximum(jnp.sum(e_c, axis=-1, keepdims=True), 1e-30)
    live = t_row[:CMP_LEN, :] >= CMP_LEN - 1
    p_c = jnp.concatenate(
        [piece for g in range(GROUP_SIZE)
         for piece in (jnp.where(live, p_c[g * tile:g * tile + CMP_LEN], 0.0),
                       p_c[g * tile + CMP_LEN:(g + 1) * tile])],
        axis=0)
    o_c = _dot(p_c.astype(BF16), vc_ref[...])

    psum = p_c[0:tile]
    for g in range(1, GROUP_SIZE):
        psum = psum + p_c[g * tile:(g + 1) * tile]
    p_hi = psum.astype(BF16)
    p_lo = (psum - p_hi.astype(F32)).astype(BF16)
    ovl = ovl_ref[...]
    imp = _dot_nt(ovl, p_hi) + _dot_nt(ovl, p_lo)
    n_blk = imp.shape[0]
    blk = lax.broadcasted_iota(jnp.int32, imp.shape, 0)
    t_q = q0 + lax.broadcasted_iota(jnp.int32, imp.shape, 1)
    cur = t_q // SEL_BLOCK
    valid = blk * SEL_BLOCK <= t_q
    forced = (blk == 0) | (blk == cur) | (blk == cur - 1)
    score = jnp.where(valid, imp + jnp.where(forced, SEL_FORCE, 0.0), NEG)
    sub = 8
    groups = [score[r:r + sub] for r in range(0, n_blk, sub)]
    ranks = [jnp.zeros_like(g) for g in groups]
    row = lax.broadcasted_iota(jnp.int32, groups[0].shape, 0)
    for m in range(n_blk):
        other = score[m:m + 1, :]
        for r, g in enumerate(groups):
            if r < m // sub:
                ranks[r] = ranks[r] + jnp.where(other > g, 1.0, 0.0)
            elif r > m // sub:
                ranks[r] = ranks[r] + jnp.where(other >= g, 1.0, 0.0)
            else:
                tie = jnp.where(row > m % sub, 1.0, 0.0)
                ranks[r] = ranks[r] + jnp.where(other > g, 1.0, 0.0) + jnp.where(other == g, tie, 0.0)
    rank = jnp.concatenate(ranks, axis=0)
    sel_neg = jnp.where((rank < float(min(N_SEL, n_blk))) & valid, 0.0, NEG)
    pad_lo = jnp.zeros((HEAD_DIM, tile), F32)
    pad_hi = jnp.zeros((LANES - HEAD_DIM - n_blk, tile), F32)
    sel_rows = jnp.concatenate([pad_lo, sel_neg, pad_hi], axis=0).T
    for g in range(GROUP_SIZE):
        qa_ref[:, g * LANES:(g + 1) * LANES] = (parts[g] + sel_rows).astype(BF16)

    gate = _expand_gates(gate_ref[...], gx_ref[...])
    for j in range(GROUP_SIZE // 2):
        pair = _merge_head_pair(o_c[2 * j * tile:(2 * j + 1) * tile], o_c[(2 * j + 1) * tile:(2 * j + 2) * tile], lane)
        oc_ref[:, j * LANES:(j + 1) * LANES] = (gate[:, j * LANES:(j + 1) * LANES] * pair).astype(BF16)


def _nsa_select(q, gates, kcv, ovl_t, gate_expand, batch, seq):
    tile = SEL_TILE
    n_t = seq // tile
    n_cmp_pad = kcv.shape[0] // batch
    grp = GROUP_SIZE * HEAD_DIM
    q_map = lambda b, h, i: (b * n_t + i, h)
    return pl.pallas_call(
        _nsa_select_kernel,
        grid=(batch, N_KV_HEADS, n_t),
        in_specs=[
            pl.BlockSpec((tile, grp), q_map),
            pl.BlockSpec((tile, LANES), q_map),
            pl.BlockSpec((n_cmp_pad, LANES), lambda b, h, i: (b, 2 * h)),
            pl.BlockSpec((n_cmp_pad, LANES), lambda b, h, i: (b, 2 * h + 1)),
            pl.BlockSpec(ovl_t.shape, lambda b, h, i: (0, 0)),
            pl.BlockSpec(gate_expand.shape, lambda b, h, i: (0, 0)),
        ],
        out_specs=[pl.BlockSpec((tile, GROUP_SIZE * LANES), q_map), pl.BlockSpec((tile, grp), q_map)],
        out_shape=[jax.ShapeDtypeStruct((batch * seq, N_HEADS * LANES), BF16),
                   jax.ShapeDtypeStruct((batch * seq, ATTN_WIDTH), BF16)],
        compiler_params=pltpu.CompilerParams(
            dimension_semantics=("arbitrary", "arbitrary", "arbitrary"), vmem_limit_bytes=VMEM_LIMIT),
        name="nsa_select",
    )(q, gates, kcv, kcv, ovl_t, gate_expand)


def _nsa_attend_kernel(qa_ref, gate_ref, oc_ref, *refs):
    i = pl.program_id(1)
    seq = refs[0].shape[0]
    for pair in range(seq // KEY_STEP):
        pl.when(i // (KEY_STEP // Q_BLOCK) == pair)(functools.partial(
            _nsa_attend_body, pair, qa_ref, gate_ref, oc_ref, *refs))


def _nsa_attend_body(pair, qa_ref, gate_ref, oc_ref, *refs):
    kv_refs, (tri_ref, edge_ref, gx_ref, o_ref) = refs[:-4], refs[-4:]
    i = pl.program_id(1)
    q0 = i * Q_BLOCK
    lane = lax.broadcasted_iota(jnp.int32, (Q_BLOCK, LANES), 1)
    k_end = (pair + 1) * KEY_STEP
    windowed = k_end > Q_BLOCK + WINDOW
    n_win = Q_BLOCK + WINDOW if windowed else k_end
    w0 = pl.multiple_of(q0 - WINDOW, Q_BLOCK) if windowed else 0
    tri = jnp.concatenate([tri_ref[i % (KEY_STEP // Q_BLOCK)]] * 2, axis=0)
    diag = tri_ref[KEY_STEP // Q_BLOCK - 1][:, KEY_STEP - Q_BLOCK:]
    diag = jnp.concatenate([diag] * 2, axis=0)
    edge = jnp.concatenate([edge_ref[...]] * 2, axis=0)
    gexp = _expand_gates(gate_ref[...], gx_ref[...])
    gates = [gexp[:, :ATTN_WIDTH], gexp[:, ATTN_WIDTH:]]

    def causal(s):
        past = s.shape[1] - KEY_STEP
        return jnp.concatenate([s[:, :past], s[:, past:] + tri], axis=1) if past else s + tri

    def queries(h, j):
        return jnp.concatenate(
            [qa_ref[:, (h * GROUP_SIZE + g) * LANES:(h * GROUP_SIZE + g + 1) * LANES] for g in (2 * j, 2 * j + 1)],
            axis=0)

    def score_mats(branch, h, j):
        ks_ref, _, kw_ref, _ = kv_refs[4 * h:4 * h + 4]
        if branch == "sel":
            return causal(_dot_nt(queries(h, j), ks_ref[0:k_end, :]))
        s = _dot_nt(queries(h, j), kw_ref[pl.ds(w0, n_win), :])
        if not windowed:
            return causal(s)
        return jnp.concatenate([s[:, :Q_BLOCK] + edge, s[:, Q_BLOCK:WINDOW], s[:, WINDOW:] + diag], axis=1)

    def numerators(s):
        return jnp.exp2(s - jnp.max(s, axis=-1, keepdims=True)).astype(BF16)

    def weighted(branch, h, probs):
        _, vs_ref, _, vw_ref = kv_refs[4 * h:4 * h + 4]
        return _dot(probs, vs_ref[0:k_end, :] if branch == "sel" else vw_ref[pl.ds(w0, n_win), :])

    def normalised(acc, gate):
        even, odd = acc[:Q_BLOCK], acc[Q_BLOCK:]
        num = jnp.where(lane < HEAD_DIM, even, pltpu.roll(odd, HEAD_DIM, 1))
        den = jnp.where(lane < HEAD_DIM, pltpu.roll(even, HEAD_DIM, 1), odd)
        return num * (gate / jnp.maximum(den, 1e-30))

    def merge(h, j, acc_w, acc_s):
        cols = slice((h * (GROUP_SIZE // 2) + j) * LANES, (h * (GROUP_SIZE // 2) + j + 1) * LANES)
        merged = (oc_ref[:, cols].astype(F32) + normalised(acc_s, gates[0][:, cols])
                  + normalised(acc_w, gates[1][:, cols]))
        o_ref[:, cols] = merged.astype(BF16)

    chains = [(branch, h, j) for h in range(N_KV_HEADS) for j in range(GROUP_SIZE // 2)
              for branch in ("win", "sel")]
    scores, probs, outs = {}, {}, {}
    for step in range(len(chains) + 2):
        if step < len(chains):
            scores[step] = score_mats(*chains[step])
        if 0 <= step - 1 < len(chains):
            probs[step - 1] = numerators(scores.pop(step - 1))
        if 0 <= step - 2 < len(chains):
            branch, h, j = chains[step - 2]
            outs[branch] = weighted(branch, h, probs.pop(step - 2))
            if branch == "sel":
                merge(h, j, outs.pop("win"), outs.pop("sel"))


def _nsa_attend(qa, gates, oc, kv8, tri, edge, gate_expand, batch, seq):
    n_qb = seq // Q_BLOCK
    q_map = lambda b, i: (b * n_qb + i, 0)
    fixed = lambda b, i: (0, 0)
    fixed3 = lambda b, i: (0, 0, 0)

    def kv_spec(slab):
        return pl.BlockSpec((seq, LANES), lambda b, i: (b, slab))

    kv_slabs = [branch * 2 * N_KV_HEADS + 2 * h + part
                for h in range(N_KV_HEADS) for branch in range(2) for part in range(2)]
    return pl.pallas_call(
        _nsa_attend_kernel,
        grid=(batch, n_qb),
        in_specs=[
            pl.BlockSpec((Q_BLOCK, N_HEADS * LANES), q_map),
            pl.BlockSpec((Q_BLOCK, N_KV_HEADS * LANES), q_map),
            pl.BlockSpec((Q_BLOCK, ATTN_WIDTH), q_map),
            *[kv_spec(s) for s in kv_slabs],
            pl.BlockSpec(tri.shape, fixed3),
            pl.BlockSpec(edge.shape, fixed),
            pl.BlockSpec(gate_expand.shape, fixed),
        ],
        out_specs=pl.BlockSpec((Q_BLOCK, ATTN_WIDTH), q_map),
        out_shape=jax.ShapeDtypeStruct((batch * seq, ATTN_WIDTH), BF16),
        compiler_params=pltpu.CompilerParams(
            dimension_semantics=("arbitrary", "arbitrary"), vmem_limit_bytes=VMEM_LIMIT),
        name="nsa_attend",
    )(qa, gates, oc, *([kv8] * len(kv_slabs)), tri, edge, gate_expand)


def _mix_ffn_ple_kernel(cx_ref, cxh_ref, bg_ref, y_ref, x_ref, p_ref, wc_ref, gc_ref, ga_ref, wo_ref, gf_ref,
                        wu_ref, cu_ref, wd_ref, gp_ref, wpg_ref, wpp_ref, gfin_ref,
                        out_ref, act_ref, n_prev_ref, *, tiles_per_seq):
    step = pl.program_id(0)
    seq_start = step % tiles_per_seq == 0

    @pl.when(step == 0)
    def _():
        n_prev_ref[...] = jnp.zeros_like(n_prev_ref)

    cx = cx_ref[...].astype(F32)
    s1, s2 = _shift_rows(cx, cxh_ref[...].astype(F32), seq_start)
    wc = wc_ref[...]
    y_conv = bg_ref[...].astype(F32) * (wc[0:1] * s2 + wc[1:2] * s1 + wc[2:3] * cx)
    mixed = jnp.concatenate(
        [_rms(y_conv, gc_ref[...]).astype(BF16), _rms(y_ref[...].astype(F32), ga_ref[...]).astype(BF16)], axis=1)
    h = x_ref[...] + _dot(mixed, wo_ref[...])

    n = _rms(h, gf_ref[...]).astype(BF16)
    nh = n_prev_ref[...]
    n_prev_ref[...] = n[n.shape[0] - BF16_ROWS:, :]

    n_ext = jnp.concatenate([nh, n], axis=0)

    def conv_up(lo):
        a_ext = _dot(n_ext, wu_ref[:, lo:lo + TF_FFN])
        a = a_ext[BF16_ROWS:]
        t1, t2 = _shift_rows(a, a_ext[:BF16_ROWS], seq_start)
        c = cu_ref[:, lo:lo + TF_FFN]
        return c[0:1] * t2 + c[1:2] * t1 + c[2:3] * a

    for j in range(D_FF // TF_FFN):
        u_gate = conv_up(j * TF_FFN)
        u_val = conv_up(D_FF + j * TF_FFN)
        act_ref[:, j * TF_FFN:(j + 1) * TF_FFN] = (u_gate * _sigmoid(u_gate) * u_val).astype(BF16)
    h = h + _dot(act_ref[...], wd_ref[...])

    gate = _sigmoid(_dot(_rms(h, gp_ref[...]).astype(BF16), wpg_ref[...]))
    h = h + gate * _dot(p_ref[...].astype(BF16), wpp_ref[...])
    out_ref[...] = _rms(h, gfin_ref[...])


def _mix_ffn_ple(cx, bg, y_attn, x2, p2, w_conv, g_conv, g_attn, w_out, g_ffn, w_up, w_ffn_conv, w_down,
                 g_ple, w_gate, w_proj, g_final, seq):
    t = x2.shape[0]
    tm = TM_FFN
    row = lambda i: (i, 0)
    halo = lambda i: (jnp.maximum(i * (tm // BF16_ROWS) - 1, 0), 0)

    def resident(a):
        return pl.BlockSpec(a.shape, lambda i: (0, 0), pipeline_mode=pl.Buffered(1))

    weights = (w_conv, g_conv, g_attn, w_out, g_ffn, w_up, w_ffn_conv, w_down, g_ple, w_gate, w_proj, g_final)
    return pl.pallas_call(
        functools.partial(_mix_ffn_ple_kernel, tiles_per_seq=seq // tm),
        grid=(t // tm,),
        in_specs=[
            pl.BlockSpec((tm, CONV_WIDTH), row),
            pl.BlockSpec((BF16_ROWS, CONV_WIDTH), halo),
            pl.BlockSpec((tm, CONV_WIDTH), row),
            pl.BlockSpec((tm, ATTN_WIDTH), row),
            pl.BlockSpec((tm, D_MODEL), row),
            pl.BlockSpec((tm, PLE_DIM), row),
            *[resident(w) for w in weights],
        ],
        out_specs=pl.BlockSpec((tm, D_MODEL), row),
        out_shape=jax.ShapeDtypeStruct((t, D_MODEL), F32),
        scratch_shapes=[pltpu.VMEM((tm, D_FF), BF16), pltpu.VMEM((BF16_ROWS, D_MODEL), BF16)],
        compiler_params=pltpu.CompilerParams(
            dimension_semantics=("arbitrary",), vmem_limit_bytes=VMEM_LIMIT),
        name="mix_ffn_ple",
    )(cx, cx, bg, y_attn, x2, p2, *weights)


def _gate_weights(w_in):
    per_kv = GROUP_SIZE * N_BRANCH
    assert w_in.shape[1] == COL_GATE + N_KV_HEADS * per_kv
    pad = jnp.zeros((w_in.shape[0], LANES - per_kv), w_in.dtype)
    slabs = []
    for h in range(N_KV_HEADS):
        slabs += [w_in[:, COL_GATE + h * per_kv:COL_GATE + (h + 1) * per_kv], pad]
    return jnp.concatenate(slabs, axis=1).astype(BF16)


def _rope_tables(pos):
    inv = ROPE_THETA ** (-jnp.arange(0, HEAD_DIM, 2, dtype=F32) / HEAD_DIM)
    ang = pos.astype(F32)[:, None] * inv[None, :]
    cos = jnp.concatenate([jnp.cos(ang), jnp.cos(ang)], axis=-1)
    sin = jnp.concatenate([-jnp.sin(ang), jnp.sin(ang)], axis=-1)
    return cos, sin


def _compress_weights(w1_k, w1_v, w2_k, w2_v, pe_k, pe_v):
    n_str = 2 * N_KV_HEADS
    half = CMP_LEN // 2
    w1 = jnp.stack([w1_k, w1_k, w1_v, w1_v]).astype(BF16)
    w1 = w1.reshape(n_str, 2, half, HEAD_DIM, HEAD_DIM)
    eye = jnp.eye(n_str, dtype=BF16)
    w1_big = jnp.einsum('chrde,cx->rcdhxe', w1, eye).reshape(half * n_str * HEAD_DIM, 2 * n_str * HEAD_DIM)
    pe = jnp.stack([pe_k, pe_k, pe_v, pe_v]).reshape(n_str, 2, half, HEAD_DIM)
    pe_big = jnp.transpose(pe, (1, 2, 0, 3)).reshape(2, half * n_str * HEAD_DIM)
    pe_big = jnp.concatenate([pe_big, jnp.zeros((6, pe_big.shape[1]), F32)], axis=0)
    w2 = jnp.stack([w2_k, w2_k, w2_v, w2_v])
    out_pos = np.asarray([0, 4, 2, 6])
    place = jnp.zeros((n_str, 2 * n_str), F32).at[np.arange(n_str), out_pos].set(1.0)
    w2_big = jnp.einsum('cde,cx->cdxe', w2, place).reshape(n_str * HEAD_DIM, 2 * n_str * HEAD_DIM)
    return w1_big.astype(BF16), pe_big.astype(BF16), w2_big.astype(BF16)


def kernel(x, p, g_mix, w_in, w_conv_mix, cmp_pe_k, cmp_w1_k, cmp_w2_k, cmp_pe_v, cmp_w1_v, cmp_w2_v,
           g_gn_conv, g_gn_attn, w_out, g_ffn, w_up, w_ffn_conv, w_down, g_ple, w_ple_gate, w_ple_proj,
           g_final):
    batch, seq, _ = x.shape
    depth = w_in.shape[0]
    t = batch * seq
    n_cmp_pad = seq // CMP_STRIDE
    n_blk = seq // SEL_BLOCK

    cos, sin = _rope_tables(jnp.arange(seq, dtype=jnp.int32))
    cos_q = jnp.concatenate([cos, cos], axis=-1)
    sin_q = jnp.concatenate([sin, sin], axis=-1)
    cmp_start = jnp.arange(n_cmp_pad) * CMP_STRIDE
    cos_e, sin_e = _rope_tables(cmp_start + CMP_LEN - 1)
    ones = jnp.ones_like(cos_e)
    zeros = jnp.zeros_like(sin_e)
    reps = CMP_ROWS // n_cmp_pad
    cos_c = jnp.tile(jnp.concatenate([cos_e, ones, ones, ones] * N_KV_HEADS, axis=-1), (reps, 1))
    sin_c = jnp.tile(jnp.concatenate([sin_e, zeros, zeros, zeros] * N_KV_HEADS, axis=-1), (reps, 1))
    blk_start = jnp.arange(n_blk) * SEL_BLOCK
    n_cmp = (seq - CMP_LEN) // CMP_STRIDE + 1
    ovl = (jnp.clip(jnp.minimum(cmp_start[:, None] + CMP_LEN, blk_start[None, :] + SEL_BLOCK)
                    - jnp.maximum(cmp_start[:, None], blk_start[None, :]), 0, None).astype(F32) / CMP_LEN)
    ovl = jnp.where(jnp.arange(n_cmp_pad)[:, None] < n_cmp, ovl, 0.0)
    ovl_t = ovl.T.astype(BF16)
    lane = jnp.arange(LANES)[None, :]
    blk_hot = (lane - HEAD_DIM == jnp.arange(seq)[:, None] // SEL_BLOCK).astype(F32)
    qi = jnp.arange(Q_BLOCK)[None, :, None]
    n_pos = KEY_STEP // Q_BLOCK
    tri = jnp.where(jnp.arange(KEY_STEP)[None, None, :] <= jnp.arange(n_pos)[:, None, None] * Q_BLOCK + qi,
                    0.0, NEG).astype(F32)
    edge = jnp.where(jnp.arange(Q_BLOCK)[None, :] > jnp.arange(Q_BLOCK)[:, None], 0.0, NEG).astype(F32)

    g_lane = jnp.arange(LANES)[:, None]
    out_head = jnp.arange(GROUP_SIZE * HEAD_DIM)[None, :] // HEAD_DIM
    gx_cmp = (g_lane == out_head * N_BRANCH).astype(BF16)
    slab_lane = jnp.arange(N_KV_HEADS * LANES)[:, None]
    out_h = jnp.arange(ATTN_WIDTH)[None, :] // HEAD_DIM
    src_lane = (out_h // GROUP_SIZE) * LANES + (out_h % GROUP_SIZE) * N_BRANCH
    gx_att = jnp.concatenate([(slab_lane == src_lane + r).astype(BF16) for r in (1, 2)], axis=1)

    pad_rows = lambda w: jnp.concatenate([w, jnp.zeros((8 - w.shape[0], w.shape[1]), w.dtype)], axis=0)

    h = x.reshape(t, D_MODEL)
    for i in range(depth):
        cx, bg, q, kvc, kv8, gates = _inproj(h, g_mix[i][None, :], w_in[i].astype(BF16), _gate_weights(w_in[i]),
                                             cos_q, sin_q, blk_hot, seq)

        w1_big, pe_big, w2_big = _compress_weights(
            cmp_w1_k[i], cmp_w1_v[i], cmp_w2_k[i], cmp_w2_v[i], cmp_pe_k[i], cmp_pe_v[i])
        kcv = _compress(kvc, w1_big, pe_big, w2_big, cos_c, sin_c)

        qa, oc = _nsa_select(q, gates, kcv, ovl_t, gx_cmp, batch, seq)
        y_attn = _nsa_attend(qa, gates, oc, kv8, tri, edge, gx_att, batch, seq)

        assert depth == 1
        h = _mix_ffn_ple(cx, bg, y_attn, h, p[i].reshape(t, PLE_DIM), pad_rows(w_conv_mix[i]),
                         g_gn_conv[i][None, :], g_gn_attn[i][None, :], w_out[i].astype(BF16), g_ffn[i][None, :],
                         w_up[i].astype(BF16), pad_rows(w_ffn_conv[i]), w_down[i].astype(BF16),
                         g_ple[i][None, :], w_ple_gate[i].astype(BF16), w_ple_proj[i].astype(BF16),
                         g_final[None, :], seq)
    return h.reshape(batch, seq, D_MODEL)
```
